```python
import jax, jax.numpy as jnp
from jax import lax
import numpy as np

D_MODEL = 2048
BATCH = 4
SEQ = 2048
DEPTH = 1

CHUNK = 64
HGRN_WIDTH = D_MODEL // 2
HGRN_HEAD_DIM = 128
HGRN_HEADS = HGRN_WIDTH // HGRN_HEAD_DIM
GMLP_WIDTH = D_MODEL // 2
GMLP_CHUNK = 128
GMLP_HEADS = 8
GMLP_HEAD_DIM = GMLP_WIDTH // GMLP_HEADS
D_FF = 4 * D_MODEL
PLE_DIM = 256
EPS = 1e-6

kernel_name = "hybrid_hgrn2_gmlp_gated_block"


def rmsnorm(x, g):
    xf = x.astype(jnp.float32)
    y = xf * lax.rsqrt(jnp.mean(xf * xf, axis=-1, keepdims=True) + EPS)
    return (y * g.astype(jnp.float32)).astype(x.dtype)


def layernorm(x, g, b):
    xf = x.astype(jnp.float32)
    mu = jnp.mean(xf, axis=-1, keepdims=True)
    var = jnp.mean(jnp.square(xf - mu), axis=-1, keepdims=True)
    y = (xf - mu) * lax.rsqrt(var + EPS)
    return (y * g.astype(jnp.float32) + b.astype(jnp.float32)).astype(x.dtype)


def hgrn2_scan(q, k, v, logf):
    B, S, H, d = q.shape
    nC = S // CHUNK

    def to_chunks(a):
        return a.reshape(B, nC, CHUNK, H, d).transpose(1, 0, 3, 2, 4)

    xs = (to_chunks(q), to_chunks(k), to_chunks(v), to_chunks(logf))
    causal = jnp.tril(jnp.ones((CHUNK, CHUNK), dtype=bool))[None, None, :, :, None]

    def step(state, inp):
        qc, kc, vc, gc = inp
        b = jnp.cumsum(gc, axis=2)
        inter = jnp.einsum('bhtd,bhde->bhte', qc * jnp.exp(b), state)
        diff = b[:, :, :, None, :] - b[:, :, None, :, :]
        decay = jnp.exp(jnp.where(causal, diff, -jnp.inf))
        attn = jnp.einsum('bhtd,bhsd,bhtsd->bhts', qc, kc, decay)
        intra = jnp.einsum('bhts,bhse->bhte', attn, vc)
        b_last = b[:, :, -1:, :]
        new_state = (jnp.exp(b_last[:, :, 0, :])[..., None] * state
                     + jnp.einsum('bhsd,bhse->bhde', kc * jnp.exp(b_last - b), vc))
        return new_state, inter + intra

    state0 = jnp.zeros((B, H, d, d), jnp.float32)
    _, out = lax.scan(step, state0, xs)
    return out.transpose(1, 0, 3, 2, 4).reshape(B, S, H, d)


def gmlp_spatial(v, w_s, b_s):
    B, S, _ = v.shape
    nG = S // GMLP_CHUNK
    pos = jnp.arange(GMLP_CHUNK)
    mask = (pos[None, :] // CHUNK) <= (pos[:, None] // CHUNK)
    w_eff = jnp.where(mask[None], w_s, jnp.zeros_like(w_s))
    vg = v.reshape(B, nG, GMLP_CHUNK, GMLP_HEADS, GMLP_HEAD_DIM)
    sv = jnp.einsum('hts,bgshc->bgthc', w_eff, vg) + b_s.T[:, :, None]
    return sv.reshape(B, S, GMLP_WIDTH)


def setup_inputs(seed: int = 0) -> dict:
    key = jax.random.key(seed)
    ks = jax.random.split(key, 24)
    f32 = jnp.float32
    n_in = 4 * HGRN_WIDTH + 2 * GMLP_WIDTH + 2 * D_MODEL

    def nrm(k, shape, fan_in):
        return jax.random.normal(k, shape, f32) * (fan_in ** -0.5)

    def gain(k, shape):
        return 1.0 + 0.02 * jax.random.normal(k, shape, f32)

    return {
        "x": jax.random.normal(ks[0], (BATCH, SEQ, D_MODEL), f32),
        "p": jax.random.normal(ks[1], (DEPTH, BATCH, SEQ, PLE_DIM), f32),
        "norm_mix": gain(ks[2], (DEPTH, D_MODEL)),
        "w_in": nrm(ks[3], (DEPTH, D_MODEL, n_in), D_MODEL),
        "lb_logits": 0.1 * jax.random.normal(ks[4], (DEPTH + 1, HGRN_WIDTH), f32),
        "hgrn_norm": gain(ks[5], (DEPTH, HGRN_HEAD_DIM)),
        "w_a_out": nrm(ks[6], (DEPTH, HGRN_WIDTH, D_MODEL), HGRN_WIDTH),
        "gmlp_ln_g": gain(ks[7], (DEPTH, GMLP_WIDTH)),
        "gmlp_ln_b": 0.02 * jax.random.normal(ks[8], (DEPTH, GMLP_WIDTH), f32),
        "w_spatial": nrm(ks[9], (DEPTH, GMLP_HEADS, GMLP_CHUNK, GMLP_CHUNK), GMLP_CHUNK),
        "b_spatial": gain(ks[10], (DEPTH, GMLP_HEADS, GMLP_CHUNK)),
        "w_b_out": nrm(ks[11], (DEPTH, GMLP_WIDTH, D_MODEL), GMLP_WIDTH),
        "w_o": nrm(ks[12], (DEPTH, D_MODEL, D_MODEL), D_MODEL),
        "norm_ffn": gain(ks[13], (DEPTH, D_MODEL)),
        "w_ff1": nrm(ks[14], (DEPTH, D_MODEL, D_FF), D_MODEL),
        "w_ff2": nrm(ks[15], (DEPTH, D_FF, D_MODEL), D_FF),
        "norm_ple": gain(ks[16], (DEPTH, D_MODEL)),
        "w_ple_gate": nrm(ks[17], (DEPTH, D_MODEL, D_MODEL), D_MODEL),
        "w_ple_proj": nrm(ks[18], (DEPTH, PLE_DIM, D_MODEL), PLE_DIM),
        "norm_final": gain(ks[19], (D_MODEL,)),
    }


def reference(x, p, norm_mix, w_in, lb_logits, hgrn_norm, w_a_out, gmlp_ln_g, gmlp_ln_b,
              w_spatial, b_spatial, w_b_out, w_o, norm_ffn, w_ff1, w_ff2, norm_ple,
              w_ple_gate, w_ple_proj, norm_final):
    B, S, _ = x.shape
    f32 = jnp.float32
    split_sizes = [HGRN_WIDTH] * 4 + [GMLP_WIDTH] * 2 + [D_MODEL] * 2
    split_idx = [int(s) for s in np.cumsum(split_sizes)[:-1]]
    lower_bounds = jnp.cumsum(jax.nn.softmax(lb_logits.astype(f32), axis=0), axis=0)

    for i in range(DEPTH):
        h = rmsnorm(x, norm_mix[i])
        proj = h @ w_in[i]
        q, f_pre, inp, g, u, v, gate_a, gate_b = jnp.split(proj, split_idx, axis=-1)

        lb = lower_bounds[i]
        f = lb + (1.0 - lb) * jax.nn.sigmoid(f_pre.astype(f32))
        k = 1.0 - f
        heads = lambda a: a.reshape(B, S, HGRN_HEADS, HGRN_HEAD_DIM)
        o = hgrn2_scan(heads(jax.nn.silu(q.astype(f32))), heads(k),
                       heads(inp.astype(f32)), heads(jnp.log(f)))
        o = rmsnorm(o, hgrn_norm[i]).astype(x.dtype) * jax.nn.silu(heads(g))
        y_a = o.reshape(B, S, HGRN_WIDTH) @ w_a_out[i]

        u = jax.nn.gelu(u, approximate=False)
        v = layernorm(jax.nn.gelu(v, approximate=False), gmlp_ln_g[i], gmlp_ln_b[i])
        y_b = (u * gmlp_spatial(v, w_spatial[i], b_spatial[i])) @ w_b_out[i]

        merged = jax.nn.sigmoid(gate_a) * y_a + jax.nn.sigmoid(gate_b) * y_b
        x = x + merged @ w_o[i]

        hf = rmsnorm(x, norm_ffn[i])
        x = x + jnp.square(jax.nn.relu(hf @ w_ff1[i])) @ w_ff2[i]

        gate_p = jax.nn.sigmoid(rmsnorm(x, norm_ple[i]) @ w_ple_gate[i])
        x = x + gate_p * (p[i] @ w_ple_proj[i])

    return rmsnorm(x, norm_final)
```

```python
import functools

import jax
import jax.numpy as jnp
import numpy as np
from jax import lax
from jax.experimental import pallas as pl
from jax.experimental.pallas import tpu as pltpu

F32 = jnp.float32
BF16 = jnp.bfloat16

D_MODEL = 2048
WIDTH = D_MODEL // 2
HEAD_DIM = 128
HEADS = WIDTH // HEAD_DIM
CHUNK = 64
GMLP_CHUNK = 128
D_FF = 4 * D_MODEL
PLE_DIM = 256
EPS = 1e-6
N_IN = 6 * WIDTH + 2 * D_MODEL

SCAN_LEVELS = 6
VMEM_LIMIT_BYTES = 56 * 1024 * 1024

SEG_Q, SEG_F, SEG_INP, SEG_G, SEG_U, SEG_V, SEG_GATE = 0, 1, 2, 3, 4, 5, 6


def _rms(x):
    return x * lax.rsqrt(jnp.mean(x * x, axis=-1, keepdims=True) + EPS)


def _dot(a, b):
    return jnp.dot(a, b, preferred_element_type=F32)


def _dot_nt(a, b):
    return lax.dot_general(a, b, (((1,), (1,)), ((), ())), preferred_element_type=F32)


def _gelu(x):
    return x * (lax.erf(x * np.float32(1.0 / np.sqrt(2.0))) + 1.0) * 0.5


def _params(*semantics):
    return pltpu.CompilerParams(
        dimension_semantics=semantics, vmem_limit_bytes=VMEM_LIMIT_BYTES)


def _inproj_kernel(x_ref, gain_ref, w_ref, lbl_ref, lng_ref, lnb_ref, o_ref, h_ref):
    j = pl.program_id(1)

    @pl.when(j == 0)
    def _():
        h_ref[...] = (_rms(x_ref[...]) * gain_ref[...]).astype(BF16)

    acc = _dot(h_ref[...], w_ref[...])

    @pl.when((j == SEG_Q) | (j == SEG_G))
    def _():
        o_ref[...] = acc * jax.nn.sigmoid(acc)

    @pl.when(j == SEG_F)
    def _():
        logits = lbl_ref[...]
        e = jnp.exp(logits - jnp.max(logits, axis=0, keepdims=True))
        lb = (e / jnp.sum(e, axis=0, keepdims=True))[0:1, :]
        o_ref[...] = jnp.log(lb + (1.0 - lb) * jax.nn.sigmoid(acc))

    @pl.when(j == SEG_INP)
    def _():
        o_ref[...] = acc

    @pl.when(j == SEG_U)
    def _():
        o_ref[...] = _gelu(acc)

    @pl.when(j == SEG_V)
    def _():
        a = _gelu(acc)
        mu = jnp.mean(a, axis=-1, keepdims=True)
        c = a - mu
        var = jnp.mean(c * c, axis=-1, keepdims=True)
        o_ref[...] = c * lax.rsqrt(var + EPS) * lng_ref[...] + lnb_ref[...]

    @pl.when(j >= SEG_GATE)
    def _():
        o_ref[...] = jax.nn.sigmoid(acc)


def _inproj(x2d, gain, w_bf, lb_logits, ln_g, ln_b, *, tm):
    t = x2d.shape[0]
    row = lambda i, j: (i, 0)
    fixed = lambda i, j: (0, 0)
    return pl.pallas_call(
        _inproj_kernel,
        grid=(t // tm, N_IN // WIDTH),
        in_specs=[
            pl.BlockSpec((tm, D_MODEL), row),
            pl.BlockSpec((1, D_MODEL), fixed),
            pl.BlockSpec((D_MODEL, WIDTH), lambda i, j: (0, j)),
            pl.BlockSpec(lb_logits.shape, fixed),
            pl.BlockSpec((1, WIDTH), fixed),
            pl.BlockSpec((1, WIDTH), fixed),
        ],
        out_specs=pl.BlockSpec((tm, WIDTH), lambda i, j: (i, j)),
        out_shape=jax.ShapeDtypeStruct((t, N_IN), F32),
        scratch_shapes=[pltpu.VMEM((tm, D_MODEL), BF16)],
        compiler_params=_params("parallel", "arbitrary"),
        name="inproj",
    )(x2d, gain, w_bf, lb_logits, ln_g, ln_b)


def _scan_constants():
    t = np.arange(CHUNK)[:, None]
    j = np.arange(CHUNK)[None, :]
    sums = [j <= t, j > t]
    masks = []
    for lvl in range(SCAN_LEVELS):
        half = 1 << lvl
        r = ((t >> (lvl + 1)) << (lvl + 1)) + half - 1
        upper = ((t >> lvl) & 1) == 1
        sums.append(np.where(upper, (j > r) & (j <= t), (j > t) & (j <= r)))
        same = (t >> (lvl + 1)) == (j >> (lvl + 1))
        masks.append(same & upper & (((j >> lvl) & 1) == 0))
    masks.append(t == j)
    return (np.concatenate(sums, axis=0).astype(np.float32),
            np.stack(masks).astype(np.float32))


def _hgrn_kernel(q_ref, lf_ref, v_ref, g_ref, sel_ref, mask_ref, gn_ref, o_ref, st_ref):
    st_ref[...] = jnp.zeros_like(st_ref)
    sel = sel_ref[...]
    gn = gn_ref[...]

    def chunk(c, carry):
        rows = pl.ds(pl.multiple_of(c * CHUNK, CHUNK), CHUNK)
        q = q_ref[rows, :]
        lf = lf_ref[rows, :]
        v = v_ref[rows, :]
        k = 1.0 - jnp.exp(lf)

        hi = lf.astype(BF16)
        rem = lf - hi.astype(F32)
        mid = rem.astype(BF16)
        lo = (rem - mid.astype(F32)).astype(BF16)
        ex = jnp.exp(_dot(sel, hi) + _dot(sel, mid) + _dot(sel, lo))

        kb = k.astype(BF16)
        attn = jnp.where(mask_ref[SCAN_LEVELS] > 0, _dot_nt(q.astype(BF16), kb), 0.0)
        for lvl in range(SCAN_LEVELS):
            w = ex[(2 + lvl) * CHUNK:(3 + lvl) * CHUNK, :]
            s = _dot_nt((q * w).astype(BF16), (k * w).astype(BF16))
            attn = attn + jnp.where(mask_ref[lvl] > 0, s, 0.0)

        vb = v.astype(BF16)
        st = st_ref[...]
        o = _dot_nt((q * ex[0:CHUNK, :]).astype(BF16), st.astype(BF16))
        o = o + _dot(attn.astype(BF16), vb)

        k_end = (k * ex[CHUNK:2 * CHUNK, :]).astype(BF16)
        st_ref[...] = st * ex[CHUNK - 1:CHUNK, :] + _dot(v.T.astype(BF16), k_end)

        o_ref[rows, :] = (_rms(o) * gn * g_ref[rows, :]).astype(BF16)
        return carry

    lax.fori_loop(0, q_ref.shape[0] // CHUNK, chunk, 0)


def _hgrn(proj, gn, batch, seq):
    sel, masks = _scan_constants()
    blk = lambda seg: pl.BlockSpec((seq, HEAD_DIM), lambda b, h: (b, seg * HEADS + h))
    fixed2 = lambda b, h: (0, 0)
    return pl.pallas_call(
        _hgrn_kernel,
        grid=(batch, HEADS),
        in_specs=[
            blk(SEG_Q), blk(SEG_F), blk(SEG_INP), blk(SEG_G),
            pl.BlockSpec(sel.shape, fixed2),
            pl.BlockSpec(masks.shape, lambda b, h: (0, 0, 0)),
            pl.BlockSpec((1, HEAD_DIM), fixed2),
        ],
        out_specs=pl.BlockSpec((seq, HEAD_DIM), lambda b, h: (b, h)),
        out_shape=jax.ShapeDtypeStruct((batch * seq, WIDTH), BF16),
        scratch_shapes=[pltpu.VMEM((HEAD_DIM, HEAD_DIM), F32)],
        compiler_params=_params("parallel", "arbitrary"),
        name="hgrn_scan",
    )(proj, proj, proj, proj, jnp.asarray(sel, BF16), jnp.asarray(masks), gn)


def _gmlp_kernel(u_ref, v_ref, w_ref, bias_ref, o_ref):
    t = lax.broadcasted_iota(jnp.int32, (GMLP_CHUNK, GMLP_CHUNK), 0)
    s = lax.broadcasted_iota(jnp.int32, (GMLP_CHUNK, GMLP_CHUNK), 1)
    keep = (s // CHUNK) <= (t // CHUNK)
    for h in range(HEADS):
        cols = slice(h * HEAD_DIM, (h + 1) * HEAD_DIM)
        w = jnp.where(keep, w_ref[h], 0.0).astype(BF16)
        bias = bias_ref[:, cols]
        for g in range(u_ref.shape[0] // GMLP_CHUNK):
            rows = slice(g * GMLP_CHUNK, (g + 1) * GMLP_CHUNK)
            sv = _dot(w, v_ref[rows, cols].astype(BF16)) + bias
            o_ref[rows, cols] = (u_ref[rows, cols] * sv).astype(BF16)


def _gmlp(proj, w_spatial, bias_full, *, tm):
    t = proj.shape[0]
    return pl.pallas_call(
        _gmlp_kernel,
        grid=(t // tm,),
        in_specs=[
            pl.BlockSpec((tm, WIDTH), lambda i: (i, SEG_U)),
            pl.BlockSpec((tm, WIDTH), lambda i: (i, SEG_V)),
            pl.BlockSpec(w_spatial.shape, lambda i: (0, 0, 0)),
            pl.BlockSpec(bias_full.shape, lambda i: (0, 0)),
        ],
        out_specs=pl.BlockSpec((tm, WIDTH), lambda i: (i, 0)),
        out_shape=jax.ShapeDtypeStruct((t, WIDTH), BF16),
        compiler_params=_params("parallel"),
        name="gmlp_spatial",
    )(proj, proj, w_spatial, bias_full)


def _merge_kernel(a_ref, b_ref, ga_ref, gb_ref, x_ref, wa_ref, wb_ref, wo_ref, o_ref):
    ya = _dot(a_ref[...], wa_ref[...])
    yb = _dot(b_ref[...], wb_ref[...])
    merged = (ga_ref[...] * ya + gb_ref[...] * yb).astype(BF16)
    o_ref[...] = x_ref[...] + _dot(merged, wo_ref[...])


def _merge(oa, ob, proj, x2d, wa, wb, wo, *, tm):
    t = x2d.shape[0]
    row = lambda i: (i, 0)
    fixed = lambda i: (0, 0)
    gate0 = SEG_GATE * WIDTH // D_MODEL
    return pl.pallas_call(
        _merge_kernel,
        grid=(t // tm,),
        in_specs=[
            pl.BlockSpec((tm, WIDTH), row),
            pl.BlockSpec((tm, WIDTH), row),
            pl.BlockSpec((tm, D_MODEL), lambda i: (i, gate0)),
            pl.BlockSpec((tm, D_MODEL), lambda i: (i, gate0 + 1)),
            pl.BlockSpec((tm, D_MODEL), row),
            pl.BlockSpec(wa.shape, fixed),
            pl.BlockSpec(wb.shape, fixed),
            pl.BlockSpec(wo.shape, fixed),
        ],
        out_specs=pl.BlockSpec((tm, D_MODEL), row),
        out_shape=jax.ShapeDtypeStruct((t, D_MODEL), F32),
        compiler_params=_params("parallel"),
        name="merge_out",
    )(oa, ob, proj, proj, x2d, wa, wb, wo)


def _ffn_kernel(x_ref, gain_ref, w1_ref, w2_ref, o_ref, h_ref):
    j = pl.program_id(1)

    @pl.when(j == 0)
    def _():
        x = x_ref[...]
        h_ref[...] = (_rms(x) * gain_ref[...]).astype(BF16)
        o_ref[...] = x

    a = jnp.maximum(_dot(h_ref[...], w1_ref[...]), 0.0)
    o_ref[...] += _dot((a * a).astype(BF16), w2_ref[...])


def _ffn(x2d, gain, w1, w2, *, tm, tf):
    t = x2d.shape[0]
    row = lambda i, j: (i, 0)
    return pl.pallas_call(
        _ffn_kernel,
        grid=(t // tm, D_FF // tf),
        in_specs=[
            pl.BlockSpec((tm, D_MODEL), row),
            pl.BlockSpec((1, D_MODEL), lambda i, j: (0, 0)),
            pl.BlockSpec((D_MODEL, tf), lambda i, j: (0, j)),
            pl.BlockSpec((tf, D_MODEL), lambda i, j: (j, 0)),
        ],
        out_specs=pl.BlockSpec((tm, D_MODEL), row),
        out_shape=jax.ShapeDtypeStruct((t, D_MODEL), F32),
        scratch_shapes=[pltpu.VMEM((tm, D_MODEL), BF16)],
        compiler_params=_params("parallel", "arbitrary"),
        name="ffn",
    )(x2d, gain, w1, w2)


def _ple_kernel(x_ref, p_ref, gain_ref, wg_ref, wp_ref, gfin_ref, o_ref):
    x = x_ref[...]
    h = (_rms(x) * gain_ref[...]).astype(BF16)
    gate = jax.nn.sigmoid(_dot(h, wg_ref[...]))
    emb = _dot(p_ref[...].astype(BF16), wp_ref[...])
    o_ref[...] = _rms(x + gate * emb) * gfin_ref[...]


def _ple(x2d, p2d, gain, wg, wp, gfin, *, tm):
    t = x2d.shape[0]
    row = lambda i: (i, 0)
    fixed = lambda i: (0, 0)
    return pl.pallas_call(
        _ple_kernel,
        grid=(t // tm,),
        in_specs=[
            pl.BlockSpec((tm, D_MODEL), row),
            pl.BlockSpec((tm, PLE_DIM), row),
            pl.BlockSpec((1, D_MODEL), fixed),
            pl.BlockSpec(wg.shape, fixed),
            pl.BlockSpec(wp.shape, fixed),
            pl.BlockSpec((1, D_MODEL), fixed),
        ],
        out_specs=pl.BlockSpec((tm, D_MODEL), row),
        out_shape=jax.ShapeDtypeStruct((t, D_MODEL), F32),
        compiler_params=_params("parallel"),
        name="ple_final",
    )(x2d, p2d, gain, wg, wp, gfin)


def kernel(x, p, norm_mix, w_in, lb_logits, hgrn_norm, w_a_out, gmlp_ln_g, gmlp_ln_b,
           w_spatial, b_spatial, w_b_out, w_o, norm_ffn, w_ff1, w_ff2, norm_ple,
           w_ple_gate, w_ple_proj, norm_final):
    batch, seq, _ = x.shape
    depth = w_in.shape[0]
    assert depth == 1 and seq % GMLP_CHUNK == 0
    t = batch * seq
    x2d = x.reshape(t, D_MODEL)
    row = lambda a: a.reshape(1, -1)

    for i in range(depth):
        proj = _inproj(x2d, row(norm_mix[i]), w_in[i].astype(BF16), lb_logits,
                       row(gmlp_ln_g[i]), row(gmlp_ln_b[i]), tm=512)
        o_a = _hgrn(proj, row(hgrn_norm[i]), batch, seq)
        bias_full = jnp.repeat(b_spatial[i].T, HEAD_DIM, axis=1)
        o_b = _gmlp(proj, w_spatial[i], bias_full, tm=512)
        x2d = _merge(o_a, o_b, proj, x2d, w_a_out[i].astype(BF16),
                     w_b_out[i].astype(BF16), w_o[i].astype(BF16), tm=256)
        x2d = _ffn(x2d, row(norm_ffn[i]), w_ff1[i].astype(BF16),
                   w_ff2[i].astype(BF16), tm=512, tf=1024)
        x2d = _ple(x2d, p[i].reshape(t, PLE_DIM), row(norm_ple[i]),
                   w_ple_gate[i].astype(BF16), w_ple_proj[i].astype(BF16),
                   row(norm_final), tm=256)
    return x2d.reshape(batch, seq, D_MODEL)
```

```python
import functools

import jax
import jax.numpy as jnp
import numpy as np
from jax import lax
from jax.experimental import pallas as pl
from jax.experimental.pallas import tpu as pltpu

F32 = jnp.float32
BF16 = jnp.bfloat16

D_MODEL = 2048
WIDTH = D_MODEL // 2
HEAD_DIM = 128
HEADS = WIDTH // HEAD_DIM
CHUNK = 64
GMLP_CHUNK = 128
D_FF = 4 * D_MODEL
PLE_DIM = 256
EPS = 1e-6
N_IN = 6 * WIDTH + 2 * D_MODEL

SCAN_LEVELS = 6
VMEM_LIMIT_BYTES = 56 * 1024 * 1024

SEG_Q, SEG_F, SEG_INP, SEG_G, SEG_U, SEG_V, SEG_GATE = 0, 1, 2, 3, 4, 5, 6


def _rms(x):
    return x * lax.rsqrt(jnp.mean(x * x, axis=-1, keepdims=True) + EPS)


def _dot(a, b):
    return jnp.dot(a, b, preferred_element_type=F32)


def _dot_nt(a, b):
    return lax.dot_general(a, b, (((1,), (1,)), ((), ())), preferred_element_type=F32)


def _gelu(x):
    return x * (lax.erf(x * np.float32(1.0 / np.sqrt(2.0))) + 1.0) * 0.5


def _params(*semantics):
    return pltpu.CompilerParams(
        dimension_semantics=semantics, vmem_limit_bytes=VMEM_LIMIT_BYTES)


def _inproj_kernel(x_ref, gain_ref, w_ref, lbl_ref, lng_ref, lnb_ref, o_ref, h_ref):
    j = pl.program_id(1)

    @pl.when(j == 0)
    def _():
        h_ref[...] = (_rms(x_ref[...]) * gain_ref[...]).astype(BF16)

    acc = _dot(h_ref[...], w_ref[...])

    @pl.when((j == SEG_Q) | (j == SEG_G))
    def _():
        o_ref[...] = acc * jax.nn.sigmoid(acc)

    @pl.when(j == SEG_F)
    def _():
        logits = lbl_ref[...]
        e = jnp.exp(logits - jnp.max(logits, axis=0, keepdims=True))
        lb = (e / jnp.sum(e, axis=0, keepdims=True))[0:1, :]
        o_ref[...] = jnp.log(lb + (1.0 - lb) * jax.nn.sigmoid(acc))

    @pl.when(j == SEG_INP)
    def _():
        o_ref[...] = acc

    @pl.when(j == SEG_U)
    def _():
        o_ref[...] = _gelu(acc)

    @pl.when(j == SEG_V)
    def _():
        a = _gelu(acc)
        mu = jnp.mean(a, axis=-1, keepdims=True)
        c = a - mu
        var = jnp.mean(c * c, axis=-1, keepdims=True)
        o_ref[...] = c * lax.rsqrt(var + EPS) * lng_ref[...] + lnb_ref[...]

    @pl.when(j >= SEG_GATE)
    def _():
        o_ref[...] = jax.nn.sigmoid(acc)


def _inproj(x2d, gain, w_bf, lb_logits, ln_g, ln_b, *, tm):
    t = x2d.shape[0]
    row = lambda i, j: (i, 0)
    fixed = lambda i, j: (0, 0)
    return pl.pallas_call(
        _inproj_kernel,
        grid=(t // tm, N_IN // WIDTH),
        in_specs=[
            pl.BlockSpec((tm, D_MODEL), row),
            pl.BlockSpec((1, D_MODEL), fixed),
            pl.BlockSpec((D_MODEL, WIDTH), lambda i, j: (0, j)),
            pl.BlockSpec(lb_logits.shape, fixed),
            pl.BlockSpec((1, WIDTH), fixed),
            pl.BlockSpec((1, WIDTH), fixed),
        ],
        out_specs=pl.BlockSpec((tm, WIDTH), lambda i, j: (i, j)),
        out_shape=jax.ShapeDtypeStruct((t, N_IN), F32),
        scratch_shapes=[pltpu.VMEM((tm, D_MODEL), BF16)],
        compiler_params=_params("parallel", "arbitrary"),
        name="inproj",
    )(x2d, gain, w_bf, lb_logits, ln_g, ln_b)


def _pair_masks():
    t = np.arange(CHUNK)[:, None]
    s = np.arange(CHUNK)[None, :]
    masks = []
    for lvl in range(SCAN_LEVELS):
        same = (t >> (lvl + 1)) == (s >> (lvl + 1))
        masks.append(same & (((t >> lvl) & 1) == 1) & (((s >> lvl) & 1) == 0))
    masks.append(t == s)
    return np.stack(masks).astype(np.float32)


def _block_ref_row(p, lvl):
    d = p.shape[-1]
    if lvl >= 2:
        half = 1 << lvl
        p3 = p.reshape(CHUNK // (2 * half), 2 * half, d)
        return jnp.broadcast_to(p3[:, half - 1:half, :], p3.shape).reshape(CHUNK, d)
    sub = lax.broadcasted_iota(jnp.int32, (CHUNK, d), 0)
    prev1 = pltpu.roll(p, 1, axis=0)
    if lvl == 0:
        return jnp.where((sub & 1) == 1, prev1, p)
    pos = sub & 3
    nxt1 = pltpu.roll(p, CHUNK - 1, axis=0)
    prev2 = pltpu.roll(p, 2, axis=0)
    return jnp.where(pos == 0, nxt1, jnp.where(pos == 1, p, jnp.where(pos == 2, prev1, prev2)))


def _hgrn_kernel(q_ref, lf_ref, v_ref, g_ref, mask_ref, gn_ref, o_ref,
                 qs_ref, ks_ref, dec_ref, st_ref):
    n_chunks = q_ref.shape[0] // CHUNK
    row_id = lax.broadcasted_iota(jnp.int32, (CHUNK, HEAD_DIM), 0)

    def scale(c, carry):
        rows = pl.ds(pl.multiple_of(c * CHUNK, CHUNK), CHUNK)
        q = q_ref[rows, :]
        lf = lf_ref[rows, :]
        k = 1.0 - jnp.exp(lf)
        qs_ref[0, rows, :] = q.astype(BF16)
        ks_ref[0, rows, :] = k.astype(BF16)
        p = lf
        for lvl in range(SCAN_LEVELS):
            upper = ((row_id >> lvl) & 1) == 1
            ref = _block_ref_row(p, lvl)
            w = jnp.exp(jnp.where(upper, p, ref - p))
            qs_ref[1 + lvl, rows, :] = (q * w).astype(BF16)
            if lvl > 0:
                ks_ref[lvl, rows, :] = (k * w).astype(BF16)
            p = p + jnp.where(upper, ref, 0.0)
        last = jnp.broadcast_to(p[CHUNK - 1:CHUNK, :], p.shape)
        qs_ref[1 + SCAN_LEVELS, rows, :] = (q * jnp.exp(p)).astype(BF16)
        ks_ref[SCAN_LEVELS, rows, :] = (k * jnp.exp(last - p)).astype(BF16)
        dec_ref[c] = jnp.exp(last[0:8, :])
        return carry

    lax.fori_loop(0, n_chunks, scale, 0)

    st_ref[...] = jnp.zeros_like(st_ref)
    gn = gn_ref[...]

    def mix(c, carry):
        rows = pl.ds(pl.multiple_of(c * CHUNK, CHUNK), CHUNK)
        attn = mask_ref[SCAN_LEVELS] * _dot_nt(qs_ref[0, rows, :], ks_ref[0, rows, :])
        for lvl in range(SCAN_LEVELS):
            s = _dot_nt(qs_ref[1 + lvl, rows, :], ks_ref[lvl, rows, :])
            attn = attn + mask_ref[lvl] * s

        v = v_ref[rows, :]
        st = st_ref[...]
        o = _dot_nt(qs_ref[1 + SCAN_LEVELS, rows, :], st.astype(BF16))
        o = o + _dot(attn.astype(BF16), v.astype(BF16))
        upd = _dot(v.T.astype(BF16), ks_ref[SCAN_LEVELS, rows, :])
        st_ref[...] = st * dec_ref[c][0:1, :] + upd

        o_ref[rows, :] = (_rms(o) * gn * g_ref[rows, :]).astype(BF16)
        return carry

    lax.fori_loop(0, n_chunks, mix, 0, unroll=8)


def _hgrn(proj, gn, batch, seq):
    masks = _pair_masks()
    blk = lambda seg: pl.BlockSpec((seq, HEAD_DIM), lambda b, h: (b, seg * HEADS + h))
    return pl.pallas_call(
        _hgrn_kernel,
        grid=(batch, HEADS),
        in_specs=[
            blk(SEG_Q), blk(SEG_F), blk(SEG_INP), blk(SEG_G),
            pl.BlockSpec(masks.shape, lambda b, h: (0, 0, 0)),
            pl.BlockSpec((1, HEAD_DIM), lambda b, h: (0, 0)),
        ],
        out_specs=pl.BlockSpec((seq, HEAD_DIM), lambda b, h: (b, h)),
        out_shape=jax.ShapeDtypeStruct((batch * seq, WIDTH), BF16),
        scratch_shapes=[
            pltpu.VMEM((SCAN_LEVELS + 2, seq, HEAD_DIM), BF16),
            pltpu.VMEM((SCAN_LEVELS + 1, seq, HEAD_DIM), BF16),
            pltpu.VMEM((seq // CHUNK, 8, HEAD_DIM), F32),
            pltpu.VMEM((HEAD_DIM, HEAD_DIM), F32),
        ],
        compiler_params=_params("parallel", "arbitrary"),
        name="hgrn_scan",
    )(proj, proj, proj, proj, jnp.asarray(masks), gn)


def _gmlp_kernel(u_ref, v_ref, w_ref, bias_ref, o_ref):
    t = lax.broadcasted_iota(jnp.int32, (GMLP_CHUNK, GMLP_CHUNK), 0)
    s = lax.broadcasted_iota(jnp.int32, (GMLP_CHUNK, GMLP_CHUNK), 1)
    keep = (s // CHUNK) <= (t // CHUNK)
    for h in range(HEADS):
        cols = slice(h * HEAD_DIM, (h + 1) * HEAD_DIM)
        w = jnp.where(keep, w_ref[h], 0.0).astype(BF16)
        bias = bias_ref[:, cols]
        for g in range(u_ref.shape[0] // GMLP_CHUNK):
            rows = slice(g * GMLP_CHUNK, (g + 1) * GMLP_CHUNK)
            sv = _dot(w, v_ref[rows, cols].astype(BF16)) + bias
            o_ref[rows, cols] = (u_ref[rows, cols] * sv).astype(BF16)


def _gmlp(proj, w_spatial, bias_full, *, tm):
    t = proj.shape[0]
    return pl.pallas_call(
        _gmlp_kernel,
        grid=(t // tm,),
        in_specs=[
            pl.BlockSpec((tm, WIDTH), lambda i: (i, SEG_U)),
            pl.BlockSpec((tm, WIDTH), lambda i: (i, SEG_V)),
            pl.BlockSpec(w_spatial.shape, lambda i: (0, 0, 0)),
            pl.BlockSpec(bias_full.shape, lambda i: (0, 0)),
        ],
        out_specs=pl.BlockSpec((tm, WIDTH), lambda i: (i, 0)),
        out_shape=jax.ShapeDtypeStruct((t, WIDTH), BF16),
        compiler_params=_params("parallel"),
        name="gmlp_spatial",
    )(proj, proj, w_spatial, bias_full)


def _merge_kernel(a_ref, b_ref, ga_ref, gb_ref, x_ref, wa_ref, wb_ref, wo_ref, o_ref):
    ya = _dot(a_ref[...], wa_ref[...])
    yb = _dot(b_ref[...], wb_ref[...])
    merged = (ga_ref[...] * ya + gb_ref[...] * yb).astype(BF16)
    o_ref[...] = x_ref[...] + _dot(merged, wo_ref[...])


def _merge(oa, ob, proj, x2d, wa, wb, wo, *, tm):
    t = x2d.shape[0]
    row = lambda i: (i, 0)
    fixed = lambda i: (0, 0)
    gate0 = SEG_GATE * WIDTH // D_MODEL
    return pl.pallas_call(
        _merge_kernel,
        grid=(t // tm,),
        in_specs=[
            pl.BlockSpec((tm, WIDTH), row),
            pl.BlockSpec((tm, WIDTH), row),
            pl.BlockSpec((tm, D_MODEL), lambda i: (i, gate0)),
            pl.BlockSpec((tm, D_MODEL), lambda i: (i, gate0 + 1)),
            pl.BlockSpec((tm, D_MODEL), row),
            pl.BlockSpec(wa.shape, fixed),
            pl.BlockSpec(wb.shape, fixed),
            pl.BlockSpec(wo.shape, fixed),
        ],
        out_specs=pl.BlockSpec((tm, D_MODEL), row),
        out_shape=jax.ShapeDtypeStruct((t, D_MODEL), F32),
        compiler_params=_params("parallel"),
        name="merge_out",
    )(oa, ob, proj, proj, x2d, wa, wb, wo)


def _ffn_kernel(x_ref, gain_ref, w1_ref, w2_ref, o_ref, h_ref):
    j = pl.program_id(1)

    @pl.when(j == 0)
    def _():
        x = x_ref[...]
        h_ref[...] = (_rms(x) * gain_ref[...]).astype(BF16)
        o_ref[...] = x

    a = jnp.maximum(_dot(h_ref[...], w1_ref[...]), 0.0)
    o_ref[...] += _dot((a * a).astype(BF16), w2_ref[...])


def _ffn(x2d, gain, w1, w2, *, tm, tf):
    t = x2d.shape[0]
    row = lambda i, j: (i, 0)
    return pl.pallas_call(
        _ffn_kernel,
        grid=(t // tm, D_FF // tf),
        in_specs=[
            pl.BlockSpec((tm, D_MODEL), row),
            pl.BlockSpec((1, D_MODEL), lambda i, j: (0, 0)),
            pl.BlockSpec((D_MODEL, tf), lambda i, j: (0, j)),
            pl.BlockSpec((tf, D_MODEL), lambda i, j: (j, 0)),
        ],
        out_specs=pl.BlockSpec((tm, D_MODEL), row),
        out_shape=jax.ShapeDtypeStruct((t, D_MODEL), F32),
        scratch_shapes=[pltpu.VMEM((tm, D_MODEL), BF16)],
        compiler_params=_params("parallel", "arbitrary"),
        name="ffn",
    )(x2d, gain, w1, w2)


def _ple_kernel(x_ref, p_ref, gain_ref, wg_ref, wp_ref, gfin_ref, o_ref):
    x = x_ref[...]
    h = (_rms(x) * gain_ref[...]).astype(BF16)
    gate = jax.nn.sigmoid(_dot(h, wg_ref[...]))
    emb = _dot(p_ref[...].astype(BF16), wp_ref[...])
    o_ref[...] = _rms(x + gate * emb) * gfin_ref[...]


def _ple(x2d, p2d, gain, wg, wp, gfin, *, tm):
    t = x2d.shape[0]
    row = lambda i: (i, 0)
    fixed = lambda i: (0, 0)
    return pl.pallas_call(
        _ple_kernel,
        grid=(t // tm,),
        in_specs=[
            pl.BlockSpec((tm, D_MODEL), row),
            pl.BlockSpec((tm, PLE_DIM), row),
            pl.BlockSpec((1, D_MODEL), fixed),
            pl.BlockSpec(wg.shape, fixed),
            pl.BlockSpec(wp.shape, fixed),
            pl.BlockSpec((1, D_MODEL), fixed),
        ],
        out_specs=pl.BlockSpec((tm, D_MODEL), row),
        out_shape=jax.ShapeDtypeStruct((t, D_MODEL), F32),
        compiler_params=_params("parallel"),
        name="ple_final",
    )(x2d, p2d, gain, wg, wp, gfin)


def kernel(x, p, norm_mix, w_in, lb_logits, hgrn_norm, w_a_out, gmlp_ln_g, gmlp_ln_b,
           w_spatial, b_spatial, w_b_out, w_o, norm_ffn, w_ff1, w_ff2, norm_ple,
           w_ple_gate, w_ple_proj, norm_final):
    batch, seq, _ = x.shape
    depth = w_in.shape[0]
    assert depth == 1 and seq % GMLP_CHUNK == 0
    t = batch * seq
    x2d = x.reshape(t, D_MODEL)
    row = lambda a: a.reshape(1, -1)

    for i in range(depth):
        proj = _inproj(x2d, row(norm_mix[i]), w_in[i].astype(BF16), lb_logits,
                       row(gmlp_ln_g[i]), row(gmlp_ln_b[i]), tm=512)
        o_a = _hgrn(proj, row(hgrn_norm[i]), batch, seq)
        bias_full = jnp.repeat(b_spatial[i].T, HEAD_DIM, axis=1)
        o_b = _gmlp(proj, w_spatial[i], bias_full, tm=512)
        x2d = _merge(o_a, o_b, proj, x2d, w_a_out[i].astype(BF16),
                     w_b_out[i].astype(BF16), w_o[i].astype(BF16), tm=256)
        x2d = _ffn(x2d, row(norm_ffn[i]), w_ff1[i].astype(BF16),
                   w_ff2[i].astype(BF16), tm=512, tf=1024)
        x2d = _ple(x2d, p[i].reshape(t, PLE_DIM), row(norm_ple[i]),
                   w_ple_gate[i].astype(BF16), w_ple_proj[i].astype(BF16),
                   row(norm_final), tm=256)
    return x2d.reshape(batch, seq, D_MODEL)
```

```python
import jax
import jax.numpy as jnp
import numpy as np
from jax import lax
from jax.experimental import pallas as pl
from jax.experimental.pallas import tpu as pltpu

F32 = jnp.float32
BF16 = jnp.bfloat16

D_MODEL = 2048
WIDTH = D_MODEL // 2
HEAD_DIM = 128
HEADS = WIDTH // HEAD_DIM
CHUNK = 64
GMLP_CHUNK = 128
D_FF = 4 * D_MODEL
PLE_DIM = 256
EPS = 1e-6
N_IN = 6 * WIDTH + 2 * D_MODEL

SCAN_LEVELS = 6
VMEM_LIMIT_BYTES = 56 * 1024 * 1024
INPROJ_SUB = 256

SEG_Q, SEG_F, SEG_INP, SEG_G, SEG_U, SEG_V, SEG_GATE = 0, 1, 2, 3, 4, 5, 6
N32_SEGS = 2


def _rms(x):
    return x * lax.rsqrt(jnp.mean(x * x, axis=-1, keepdims=True) + EPS)


def _dot(a, b):
    return jnp.dot(a, b, preferred_element_type=F32)


def _dot_nt(a, b):
    return lax.dot_general(a, b, (((1,), (1,)), ((), ())), preferred_element_type=F32)


def _gelu(x):
    return x * (lax.erf(x * np.float32(1.0 / np.sqrt(2.0))) + 1.0) * 0.5


def _params(*semantics):
    return pltpu.CompilerParams(
        dimension_semantics=semantics, vmem_limit_bytes=VMEM_LIMIT_BYTES)


def _inproj_kernel(x_ref, gain_ref, w_ref, lbl_ref, lng_ref, lnb_ref, o32_ref, o16_ref, h_ref):
    j = pl.program_id(1)

    @pl.when(j == 0)
    def _():
        h_ref[...] = (_rms(x_ref[...]) * gain_ref[...]).astype(BF16)

    def emit(o_ref, act):
        for n in range(WIDTH // INPROJ_SUB):
            cols = slice(n * INPROJ_SUB, (n + 1) * INPROJ_SUB)
            o_ref[:, cols] = act(_dot(h_ref[...], w_ref[:, cols]), cols).astype(o_ref.dtype)

    silu = lambda a, cols: a * jax.nn.sigmoid(a)

    @pl.when(j == SEG_Q)
    def _():
        emit(o32_ref, silu)

    @pl.when(j == SEG_F)
    def _():
        logits = lbl_ref[...]
        e = jnp.exp(logits - jnp.max(logits, axis=0, keepdims=True))
        lb = (e / jnp.sum(e, axis=0, keepdims=True))[0:1, :]
        emit(o32_ref, lambda a, cols: jnp.log(
            lb[:, cols] + (1.0 - lb[:, cols]) * jax.nn.sigmoid(a)))

    @pl.when(j == SEG_INP)
    def _():
        emit(o16_ref, lambda a, cols: a)

    @pl.when(j == SEG_G)
    def _():
        emit(o16_ref, silu)

    @pl.when(j == SEG_U)
    def _():
        emit(o16_ref, lambda a, cols: _gelu(a))

    @pl.when(j == SEG_V)
    def _():
        a = _gelu(_dot(h_ref[...], w_ref[...]))
        mu = jnp.mean(a, axis=-1, keepdims=True)
        c = a - mu
        var = jnp.mean(c * c, axis=-1, keepdims=True)
        o16_ref[...] = (c * lax.rsqrt(var + EPS) * lng_ref[...] + lnb_ref[...]).astype(BF16)

    @pl.when(j >= SEG_GATE)
    def _():
        emit(o16_ref, lambda a, cols: jax.nn.sigmoid(a))


def _inproj(x2d, gain, w_bf, lb_logits, ln_g, ln_b, *, tm):
    t = x2d.shape[0]
    row = lambda i, j: (i, 0)
    fixed = lambda i, j: (0, 0)
    n32 = N32_SEGS * WIDTH
    return pl.pallas_call(
        _inproj_kernel,
        grid=(t // tm, N_IN // WIDTH),
        in_specs=[
            pl.BlockSpec((tm, D_MODEL), row),
            pl.BlockSpec((1, D_MODEL), fixed),
            pl.BlockSpec((D_MODEL, WIDTH), lambda i, j: (0, j)),
            pl.BlockSpec(lb_logits.shape, fixed),
            pl.BlockSpec((1, WIDTH), fixed),
            pl.BlockSpec((1, WIDTH), fixed),
        ],
        out_specs=[
            pl.BlockSpec((tm, WIDTH), lambda i, j: (i, jnp.minimum(j, N32_SEGS - 1))),
            pl.BlockSpec((tm, WIDTH), lambda i, j: (i, jnp.maximum(j - N32_SEGS, 0))),
        ],
        out_shape=[jax.ShapeDtypeStruct((t, n32), F32),
                   jax.ShapeDtypeStruct((t, N_IN - n32), BF16)],
        scratch_shapes=[pltpu.VMEM((tm, D_MODEL), BF16)],
        compiler_params=_params("parallel", "arbitrary"),
        name="inproj",
    )(x2d, gain, w_bf, lb_logits, ln_g, ln_b)


def _pair_masks():
    t = np.arange(CHUNK)[:, None]
    s = np.arange(CHUNK)[None, :]
    masks = []
    for lvl in range(SCAN_LEVELS):
        same = (t >> (lvl + 1)) == (s >> (lvl + 1))
        masks.append(same & (((t >> lvl) & 1) == 1) & (((s >> lvl) & 1) == 0))
    masks.append(t == s)
    return np.stack(masks).astype(np.float32)


def _block_ref_row(p, lvl):
    d = p.shape[-1]
    if lvl >= 2:
        half = 1 << lvl
        p3 = p.reshape(CHUNK // (2 * half), 2 * half, d)
        return jnp.broadcast_to(p3[:, half - 1:half, :], p3.shape).reshape(CHUNK, d)
    sub = lax.broadcasted_iota(jnp.int32, (CHUNK, d), 0)
    prev1 = pltpu.roll(p, 1, axis=0)
    if lvl == 0:
        return jnp.where((sub & 1) == 1, prev1, p)
    pos = sub & 3
    nxt1 = pltpu.roll(p, CHUNK - 1, axis=0)
    prev2 = pltpu.roll(p, 2, axis=0)
    return jnp.where(pos == 0, nxt1, jnp.where(pos == 1, p, jnp.where(pos == 2, prev1, prev2)))


def _hgrn_kernel(q_ref, lf_ref, v_ref, g_ref, mask_ref, gn_ref, o_ref,
                 qs_ref, ks_ref, dec_ref, st_ref):
    n_chunks = q_ref.shape[0] // CHUNK
    row_id = lax.broadcasted_iota(jnp.int32, (CHUNK, HEAD_DIM), 0)

    def scale(c, carry):
        rows = pl.ds(pl.multiple_of(c * CHUNK, CHUNK), CHUNK)
        q = q_ref[rows, :]
        lf = lf_ref[rows, :]
        k = 1.0 - jnp.exp(lf)
        qs_ref[0, rows, :] = q.astype(BF16)
        ks_ref[0, rows, :] = k.astype(BF16)
        p = lf
        for lvl in range(SCAN_LEVELS):
            upper = ((row_id >> lvl) & 1) == 1
            ref = _block_ref_row(p, lvl)
            w = jnp.exp(jnp.where(upper, p, ref - p))
            qs_ref[1 + lvl, rows, :] = (q * w).astype(BF16)
            if lvl > 0:
                ks_ref[lvl, rows, :] = (k * w).astype(BF16)
            p = p + jnp.where(upper, ref, 0.0)
        last = jnp.broadcast_to(p[CHUNK - 1:CHUNK, :], p.shape)
        qs_ref[1 + SCAN_LEVELS, rows, :] = (q * jnp.exp(p)).astype(BF16)
        ks_ref[SCAN_LEVELS, rows, :] = (k * jnp.exp(last - p)).astype(BF16)
        dec_ref[c] = jnp.exp(last[0:8, :])
        return carry

    lax.fori_loop(0, n_chunks, scale, 0)

    st_ref[...] = jnp.zeros_like(st_ref)
    gn = gn_ref[...]

    def mix(c, carry):
        rows = pl.ds(pl.multiple_of(c * CHUNK, CHUNK), CHUNK)
        attn = mask_ref[SCAN_LEVELS] * _dot_nt(qs_ref[0, rows, :], ks_ref[0, rows, :])
        for lvl in range(SCAN_LEVELS):
            s = _dot_nt(qs_ref[1 + lvl, rows, :], ks_ref[lvl, rows, :])
            attn = attn + mask_ref[lvl] * s

        v = v_ref[rows, :]
        st = st_ref[...]
        o = _dot_nt(qs_ref[1 + SCAN_LEVELS, rows, :], st.astype(BF16))
        o = o + _dot(attn.astype(BF16), v)
        upd = _dot(v.astype(F32).T.astype(BF16), ks_ref[SCAN_LEVELS, rows, :])
        st_ref[...] = st * dec_ref[c][0:1, :] + upd

        o_ref[rows, :] = (_rms(o) * gn * g_ref[rows, :].astype(F32)).astype(BF16)
        return carry

    lax.fori_loop(0, n_chunks, mix, 0, unroll=8)


def _hgrn(p32, p16, gn, batch, seq):
    masks = _pair_masks()
    blk = lambda seg: pl.BlockSpec((seq, HEAD_DIM), lambda b, h: (b, seg * HEADS + h))
    return pl.pallas_call(
        _hgrn_kernel,
        grid=(batch, HEADS),
        in_specs=[
            blk(SEG_Q), blk(SEG_F), blk(SEG_INP - N32_SEGS), blk(SEG_G - N32_SEGS),
            pl.BlockSpec(masks.shape, lambda b, h: (0, 0, 0)),
            pl.BlockSpec((1, HEAD_DIM), lambda b, h: (0, 0)),
        ],
        out_specs=pl.BlockSpec((seq, HEAD_DIM), lambda b, h: (b, h)),
        out_shape=jax.ShapeDtypeStruct((batch * seq, WIDTH), BF16),
        scratch_shapes=[
            pltpu.VMEM((SCAN_LEVELS + 2, seq, HEAD_DIM), BF16),
            pltpu.VMEM((SCAN_LEVELS + 1, seq, HEAD_DIM), BF16),
            pltpu.VMEM((seq // CHUNK, 8, HEAD_DIM), F32),
            pltpu.VMEM((HEAD_DIM, HEAD_DIM), F32),
        ],
        compiler_params=_params("parallel", "arbitrary"),
        name="hgrn_scan",
    )(p32, p32, p16, p16, jnp.asarray(masks), gn)


def _gmlp_kernel(u_ref, v_ref, w_ref, bias_ref, o_ref):
    t = lax.broadcasted_iota(jnp.int32, (GMLP_CHUNK, GMLP_CHUNK), 0)
    s = lax.broadcasted_iota(jnp.int32, (GMLP_CHUNK, GMLP_CHUNK), 1)
    keep = (s // CHUNK) <= (t // CHUNK)
    for h in range(HEADS):
        cols = slice(h * HEAD_DIM, (h + 1) * HEAD_DIM)
        w = jnp.where(keep, w_ref[h], 0.0).astype(BF16)
        bias = bias_ref[:, cols]
        for g in range(u_ref.shape[0] // GMLP_CHUNK):
            rows = slice(g * GMLP_CHUNK, (g + 1) * GMLP_CHUNK)
            sv = _dot(w, v_ref[rows, cols]) + bias
            o_ref[rows, cols] = (u_ref[rows, cols].astype(F32) * sv).astype(BF16)


def _gmlp(p16, w_spatial, bias_full, *, tm):
    t = p16.shape[0]
    return pl.pallas_call(
        _gmlp_kernel,
        grid=(t // tm,),
        in_specs=[
            pl.BlockSpec((tm, WIDTH), lambda i: (i, SEG_U - N32_SEGS)),
            pl.BlockSpec((tm, WIDTH), lambda i: (i, SEG_V - N32_SEGS)),
            pl.BlockSpec(w_spatial.shape, lambda i: (0, 0, 0)),
            pl.BlockSpec(bias_full.shape, lambda i: (0, 0)),
        ],
        out_specs=pl.BlockSpec((tm, WIDTH), lambda i: (i, 0)),
        out_shape=jax.ShapeDtypeStruct((t, WIDTH), BF16),
        compiler_params=_params("parallel"),
        name="gmlp_spatial",
    )(p16, p16, w_spatial, bias_full)


def _merge_kernel(a_ref, b_ref, ga_ref, gb_ref, x_ref, wa_ref, wb_ref, wo_ref, o_ref):
    ya = _dot(a_ref[...], wa_ref[...])
    yb = _dot(b_ref[...], wb_ref[...])
    merged = (ga_ref[...].astype(F32) * ya + gb_ref[...].astype(F32) * yb).astype(BF16)
    o_ref[...] = x_ref[...] + _dot(merged, wo_ref[...])


def _merge(oa, ob, p16, x2d, wa, wb, wo, *, tm):
    t = x2d.shape[0]
    row = lambda i: (i, 0)
    fixed = lambda i: (0, 0)
    gate0 = (SEG_GATE - N32_SEGS) * WIDTH // D_MODEL
    return pl.pallas_call(
        _merge_kernel,
        grid=(t // tm,),
        in_specs=[
            pl.BlockSpec((tm, WIDTH), row),
            pl.BlockSpec((tm, WIDTH), row),
            pl.BlockSpec((tm, D_MODEL), lambda i: (i, gate0)),
            pl.BlockSpec((tm, D_MODEL), lambda i: (i, gate0 + 1)),
            pl.BlockSpec((tm, D_MODEL), row),
            pl.BlockSpec(wa.shape, fixed),
            pl.BlockSpec(wb.shape, fixed),
            pl.BlockSpec(wo.shape, fixed),
        ],
        out_specs=pl.BlockSpec((tm, D_MODEL), row),
        out_shape=jax.ShapeDtypeStruct((t, D_MODEL), F32),
        compiler_params=_params("parallel"),
        name="merge_out",
    )(oa, ob, p16, p16, x2d, wa, wb, wo)


def _ffn_kernel(x_ref, gain_ref, w1_ref, w2_ref, o_ref, h_ref):
    j = pl.program_id(1)

    @pl.when(j == 0)
    def _():
        x = x_ref[...]
        h_ref[...] = (_rms(x) * gain_ref[...]).astype(BF16)
        o_ref[...] = x

    a = jnp.maximum(_dot(h_ref[...], w1_ref[...]), 0.0)
    o_ref[...] += _dot((a * a).astype(BF16), w2_ref[...])


def _ffn(x2d, gain, w1, w2, *, tm, tf):
    t = x2d.shape[0]
    row = lambda i, j: (i, 0)
    return pl.pallas_call(
        _ffn_kernel,
        grid=(t // tm, D_FF // tf),
        in_specs=[
            pl.BlockSpec((tm, D_MODEL), row),
            pl.BlockSpec((1, D_MODEL), lambda i, j: (0, 0)),
            pl.BlockSpec((D_MODEL, tf), lambda i, j: (0, j)),
            pl.BlockSpec((tf, D_MODEL), lambda i, j: (j, 0)),
        ],
        out_specs=pl.BlockSpec((tm, D_MODEL), row),
        out_shape=jax.ShapeDtypeStruct((t, D_MODEL), F32),
        scratch_shapes=[pltpu.VMEM((tm, D_MODEL), BF16)],
        compiler_params=_params("parallel", "arbitrary"),
        name="ffn",
    )(x2d, gain, w1, w2)


def _ple_kernel(x_ref, p_ref, gain_ref, wg_ref, wp_ref, gfin_ref, o_ref):
    x = x_ref[...]
    h = (_rms(x) * gain_ref[...]).astype(BF16)
    gate = jax.nn.sigmoid(_dot(h, wg_ref[...]))
    emb = _dot(p_ref[...].astype(BF16), wp_ref[...])
    o_ref[...] = _rms(x + gate * emb) * gfin_ref[...]


def _ple(x2d, p2d, gain, wg, wp, gfin, *, tm):
    t = x2d.shape[0]
    row = lambda i: (i, 0)
    fixed = lambda i: (0, 0)
    return pl.pallas_call(
        _ple_kernel,
        grid=(t // tm,),
        in_specs=[
            pl.BlockSpec((tm, D_MODEL), row),
            pl.BlockSpec((tm, PLE_DIM), row),
            pl.BlockSpec((1, D_MODEL), fixed),
            pl.BlockSpec(wg.shape, fixed),
            pl.BlockSpec(wp.shape, fixed),
            pl.BlockSpec((1, D_MODEL), fixed),
        ],
        out_specs=pl.BlockSpec((tm, D_MODEL), row),
        out_shape=jax.ShapeDtypeStruct((t, D_MODEL), F32),
        compiler_params=_params("parallel"),
        name="ple_final",
    )(x2d, p2d, gain, wg, wp, gfin)


def kernel(x, p, norm_mix, w_in, lb_logits, hgrn_norm, w_a_out, gmlp_ln_g, gmlp_ln_b,
           w_spatial, b_spatial, w_b_out, w_o, norm_ffn, w_ff1, w_ff2, norm_ple,
           w_ple_gate, w_ple_proj, norm_final):
    batch, seq, _ = x.shape
    assert w_in.shape[0] == 1 and seq % GMLP_CHUNK == 0
    t = batch * seq
    x2d = x.reshape(t, D_MODEL)
    row = lambda a: a.reshape(1, -1)

    p32, p16 = _inproj(x2d, row(norm_mix[0]), w_in[0].astype(BF16), lb_logits,
                       row(gmlp_ln_g[0]), row(gmlp_ln_b[0]), tm=512)
    o_a = _hgrn(p32, p16, row(hgrn_norm[0]), batch, seq)
    bias_full = jnp.repeat(b_spatial[0].T, HEAD_DIM, axis=1)
    o_b = _gmlp(p16, w_spatial[0], bias_full, tm=512)
    x2d = _merge(o_a, o_b, p16, x2d, w_a_out[0].astype(BF16),
                 w_b_out[0].astype(BF16), w_o[0].astype(BF16), tm=512)
    x2d = _ffn(x2d, row(norm_ffn[0]), w_ff1[0].astype(BF16),
               w_ff2[0].astype(BF16), tm=512, tf=1024)
    x2d = _ple(x2d, p[0].reshape(t, PLE_DIM), row(norm_ple[0]),
               w_ple_gate[0].astype(BF16), w_ple_proj[0].astype(BF16),
               row(norm_final), tm=256)
    return x2d.reshape(batch, seq, D_MODEL)
```

```python
import jax
import jax.numpy as jnp
import numpy as np
from jax import lax
from jax.experimental import pallas as pl
from jax.experimental.pallas import tpu as pltpu

F32 = jnp.float32
BF16 = jnp.bfloat16

D_MODEL = 2048
WIDTH = D_MODEL // 2
HEAD_DIM = 128
HEADS = WIDTH // HEAD_DIM
CHUNK = 64
GMLP_CHUNK = 128
D_FF = 4 * D_MODEL
PLE_DIM = 256
EPS = 1e-6
N_IN = 6 * WIDTH + 2 * D_MODEL

SCAN_LEVELS = 6
VMEM_LIMIT_BYTES = 56 * 1024 * 1024
INPROJ_SUB = 256
SCORES_UNROLL = 8
MIX_UNROLL = 8

SEG_Q, SEG_F, SEG_INP, SEG_G, SEG_U, SEG_V, SEG_GATE = 0, 1, 2, 3, 4, 5, 6
N32_SEGS = 2


def _rms(x):
    return x * lax.rsqrt(jnp.mean(x * x, axis=-1, keepdims=True) + EPS)


def _dot(a, b):
    return jnp.dot(a, b, preferred_element_type=F32)


def _dot_nt(a, b):
    return lax.dot_general(a, b, (((1,), (1,)), ((), ())), preferred_element_type=F32)


def _dot_tn(a, b):
    return lax.dot_general(a, b, (((0,), (0,)), ((), ())), preferred_element_type=F32)


def _gelu(x):
    return x * (lax.erf(x * np.float32(1.0 / np.sqrt(2.0))) + 1.0) * 0.5


def _params(*semantics):
    return pltpu.CompilerParams(
        dimension_semantics=semantics, vmem_limit_bytes=VMEM_LIMIT_BYTES)


def _inproj_kernel(x_ref, gain_ref, w_ref, lbl_ref, lng_ref, lnb_ref, o32_ref, o16_ref, h_ref):
    j = pl.program_id(1)

    @pl.when(j == 0)
    def _():
        h_ref[...] = (_rms(x_ref[...]) * gain_ref[...]).astype(BF16)

    def emit(o_ref, act):
        for n in range(WIDTH // INPROJ_SUB):
            cols = slice(n * INPROJ_SUB, (n + 1) * INPROJ_SUB)
            o_ref[:, cols] = act(_dot(h_ref[...], w_ref[:, cols]), cols).astype(o_ref.dtype)

    silu = lambda a, cols: a * jax.nn.sigmoid(a)

    @pl.when(j == SEG_Q)
    def _():
        emit(o32_ref, silu)

    @pl.when(j == SEG_F)
    def _():
        logits = lbl_ref[...]
        e = jnp.exp(logits - jnp.max(logits, axis=0, keepdims=True))
        lb = (e / jnp.sum(e, axis=0, keepdims=True))[0:1, :]
        emit(o32_ref, lambda a, cols: jnp.log(
            lb[:, cols] + (1.0 - lb[:, cols]) * jax.nn.sigmoid(a)))

    @pl.when(j == SEG_INP)
    def _():
        emit(o16_ref, lambda a, cols: a)

    @pl.when(j == SEG_G)
    def _():
        emit(o16_ref, silu)

    @pl.when(j == SEG_U)
    def _():
        emit(o16_ref, lambda a, cols: _gelu(a))

    @pl.when(j == SEG_V)
    def _():
        a = _gelu(_dot(h_ref[...], w_ref[...]))
        mu = jnp.mean(a, axis=-1, keepdims=True)
        c = a - mu
        var = jnp.mean(c * c, axis=-1, keepdims=True)
        o16_ref[...] = (c * lax.rsqrt(var + EPS) * lng_ref[...] + lnb_ref[...]).astype(BF16)

    @pl.when(j >= SEG_GATE)
    def _():
        emit(o16_ref, lambda a, cols: jax.nn.sigmoid(a))


def _inproj(x2d, gain, w_bf, lb_logits, ln_g, ln_b, *, tm):
    t = x2d.shape[0]
    row = lambda i, j: (i, 0)
    fixed = lambda i, j: (0, 0)
    n32 = N32_SEGS * WIDTH
    return pl.pallas_call(
        _inproj_kernel,
        grid=(t // tm, N_IN // WIDTH),
        in_specs=[
            pl.BlockSpec((tm, D_MODEL), row),
            pl.BlockSpec((1, D_MODEL), fixed),
            pl.BlockSpec((D_MODEL, WIDTH), lambda i, j: (0, j)),
            pl.BlockSpec(lb_logits.shape, fixed),
            pl.BlockSpec((1, WIDTH), fixed),
            pl.BlockSpec((1, WIDTH), fixed),
        ],
        out_specs=[
            pl.BlockSpec((tm, WIDTH), lambda i, j: (i, jnp.minimum(j, N32_SEGS - 1))),
            pl.BlockSpec((tm, WIDTH), lambda i, j: (i, jnp.maximum(j - N32_SEGS, 0))),
        ],
        out_shape=[jax.ShapeDtypeStruct((t, n32), F32),
                   jax.ShapeDtypeStruct((t, N_IN - n32), BF16)],
        scratch_shapes=[pltpu.VMEM((tm, D_MODEL), BF16)],
        compiler_params=_params("parallel", "arbitrary"),
        name="inproj",
    )(x2d, gain, w_bf, lb_logits, ln_g, ln_b)


def _pair_masks():
    t = np.arange(CHUNK)[:, None]
    s = np.arange(CHUNK)[None, :]
    masks = []
    for lvl in range(SCAN_LEVELS):
        same = (t >> (lvl + 1)) == (s >> (lvl + 1))
        masks.append(same & (((t >> lvl) & 1) == 1) & (((s >> lvl) & 1) == 0))
    masks.append(t == s)
    return np.stack(masks).astype(np.float32)


def _block_ref_row(p, lvl):
    d = p.shape[-1]
    if lvl >= 2:
        half = 1 << lvl
        p3 = p.reshape(CHUNK // (2 * half), 2 * half, d)
        return jnp.broadcast_to(p3[:, half - 1:half, :], p3.shape).reshape(CHUNK, d)
    sub = lax.broadcasted_iota(jnp.int32, (CHUNK, d), 0)
    prev1 = pltpu.roll(p, 1, axis=0)
    if lvl == 0:
        return jnp.where((sub & 1) == 1, prev1, p)
    pos = sub & 3
    nxt1 = pltpu.roll(p, CHUNK - 1, axis=0)
    prev2 = pltpu.roll(p, 2, axis=0)
    return jnp.where(pos == 0, nxt1, jnp.where(pos == 1, p, jnp.where(pos == 2, prev1, prev2)))


def _hgrn_kernel(q_ref, lf_ref, v_ref, g_ref, mask_ref, gn_ref, o_ref,
                 qs_ref, ks_ref, dec_ref, at_ref, st_ref):
    n_chunks = q_ref.shape[0] // CHUNK
    row_id = lax.broadcasted_iota(jnp.int32, (CHUNK, HEAD_DIM), 0)

    def scale(c, carry):
        rows = pl.ds(pl.multiple_of(c * CHUNK, CHUNK), CHUNK)
        q = q_ref[rows, :]
        lf = lf_ref[rows, :]
        k = 1.0 - jnp.exp(lf)
        qs_ref[0, rows, :] = q.astype(BF16)
        ks_ref[0, rows, :] = k.astype(BF16)
        p = lf
        for lvl in range(SCAN_LEVELS):
            upper = ((row_id >> lvl) & 1) == 1
            ref = _block_ref_row(p, lvl)
            w = jnp.exp(jnp.where(upper, p, ref - p))
            qs_ref[1 + lvl, rows, :] = (q * w).astype(BF16)
            if lvl > 0:
                ks_ref[lvl, rows, :] = (k * w).astype(BF16)
            p = p + jnp.where(upper, ref, 0.0)
        last = p[CHUNK - 1:CHUNK, :]
        qs_ref[1 + SCAN_LEVELS, rows, :] = (q * jnp.exp(p)).astype(BF16)
        ks_ref[SCAN_LEVELS, rows, :] = (k * jnp.exp(last - p)).astype(BF16)
        dec_ref[c] = jnp.broadcast_to(jnp.exp(last), (8, HEAD_DIM))
        return carry

    lax.fori_loop(0, n_chunks, scale, 0)

    def scores(c, st):
        rows = pl.ds(pl.multiple_of(c * CHUNK, CHUNK), CHUNK)
        attn = mask_ref[SCAN_LEVELS] * _dot_nt(qs_ref[0, rows, :], ks_ref[0, rows, :])
        for lvl in range(SCAN_LEVELS):
            s = _dot_nt(qs_ref[1 + lvl, rows, :], ks_ref[lvl, rows, :])
            attn = attn + mask_ref[lvl] * s
        at_ref[rows, :] = attn.astype(BF16)
        st_ref[c] = st.astype(BF16)
        return st * dec_ref[c][0:1, :] + _dot_tn(v_ref[rows, :], ks_ref[SCAN_LEVELS, rows, :])

    lax.fori_loop(0, n_chunks, scores, jnp.zeros((HEAD_DIM, HEAD_DIM), F32),
                  unroll=SCORES_UNROLL)

    gn = gn_ref[...]

    def mix(c, carry):
        rows = pl.ds(pl.multiple_of(c * CHUNK, CHUNK), CHUNK)
        o = _dot_nt(qs_ref[1 + SCAN_LEVELS, rows, :], st_ref[c])
        o = o + _dot(at_ref[rows, :], v_ref[rows, :])
        o_ref[rows, :] = (_rms(o) * gn * g_ref[rows, :].astype(F32)).astype(BF16)
        return carry

    lax.fori_loop(0, n_chunks, mix, 0, unroll=MIX_UNROLL)


def _hgrn(p32, p16, gn, batch, seq):
    masks = _pair_masks()
    blk = lambda seg: pl.BlockSpec((seq, HEAD_DIM), lambda b, h: (b, seg * HEADS + h))
    return pl.pallas_call(
        _hgrn_kernel,
        grid=(batch, HEADS),
        in_specs=[
            blk(SEG_Q), blk(SEG_F), blk(SEG_INP - N32_SEGS), blk(SEG_G - N32_SEGS),
            pl.BlockSpec(masks.shape, lambda b, h: (0, 0, 0)),
            pl.BlockSpec((1, HEAD_DIM), lambda b, h: (0, 0)),
        ],
        out_specs=pl.BlockSpec((seq, HEAD_DIM), lambda b, h: (b, h)),
        out_shape=jax.ShapeDtypeStruct((batch * seq, WIDTH), BF16),
        scratch_shapes=[
            pltpu.VMEM((SCAN_LEVELS + 2, seq, HEAD_DIM), BF16),
            pltpu.VMEM((SCAN_LEVELS + 1, seq, HEAD_DIM), BF16),
            pltpu.VMEM((seq // CHUNK, 8, HEAD_DIM), F32),
            pltpu.VMEM((seq, CHUNK), BF16),
            pltpu.VMEM((seq // CHUNK, HEAD_DIM, HEAD_DIM), BF16),
        ],
        compiler_params=_params("parallel", "arbitrary"),
        name="hgrn_scan",
    )(p32, p32, p16, p16, jnp.asarray(masks), gn)


def _gmlp_kernel(u_ref, v_ref, w_ref, bias_ref, o_ref):
    t = lax.broadcasted_iota(jnp.int32, (GMLP_CHUNK, GMLP_CHUNK), 0)
    s = lax.broadcasted_iota(jnp.int32, (GMLP_CHUNK, GMLP_CHUNK), 1)
    keep = (s // CHUNK) <= (t // CHUNK)
    for h in range(HEADS):
        cols = slice(h * HEAD_DIM, (h + 1) * HEAD_DIM)
        w = jnp.where(keep, w_ref[h], 0.0).astype(BF16)
        bias = bias_ref[:, cols]
        for g in range(u_ref.shape[0] // GMLP_CHUNK):
            rows = slice(g * GMLP_CHUNK, (g + 1) * GMLP_CHUNK)
            sv = _dot(w, v_ref[rows, cols]) + bias
            o_ref[rows, cols] = (u_ref[rows, cols].astype(F32) * sv).astype(BF16)


def _gmlp(p16, w_spatial, bias_full, *, tm):
    t = p16.shape[0]
    return pl.pallas_call(
        _gmlp_kernel,
        grid=(t // tm,),
        in_specs=[
            pl.BlockSpec((tm, WIDTH), lambda i: (i, SEG_U - N32_SEGS)),
            pl.BlockSpec((tm, WIDTH), lambda i: (i, SEG_V - N32_SEGS)),
            pl.BlockSpec(w_spatial.shape, lambda i: (0, 0, 0)),
            pl.BlockSpec(bias_full.shape, lambda i: (0, 0)),
        ],
        out_specs=pl.BlockSpec((tm, WIDTH), lambda i: (i, 0)),
        out_shape=jax.ShapeDtypeStruct((t, WIDTH), BF16),
        compiler_params=_params("parallel"),
        name="gmlp_spatial",
    )(p16, p16, w_spatial, bias_full)


def _merge_kernel(a_ref, b_ref, ga_ref, gb_ref, x_ref, wa_ref, wb_ref, wo_ref, o_ref):
    ya = _dot(a_ref[...], wa_ref[...])
    yb = _dot(b_ref[...], wb_ref[...])
    merged = (ga_ref[...].astype(F32) * ya + gb_ref[...].astype(F32) * yb).astype(BF16)
    o_ref[...] = x_ref[...] + _dot(merged, wo_ref[...])


def _merge(oa, ob, p16, x2d, wa, wb, wo, *, tm):
    t = x2d.shape[0]
    row = lambda i: (i, 0)
    fixed = lambda i: (0, 0)
    gate0 = (SEG_GATE - N32_SEGS) * WIDTH // D_MODEL
    return pl.pallas_call(
        _merge_kernel,
        grid=(t // tm,),
        in_specs=[
            pl.BlockSpec((tm, WIDTH), row),
            pl.BlockSpec((tm, WIDTH), row),
            pl.BlockSpec((tm, D_MODEL), lambda i: (i, gate0)),
            pl.BlockSpec((tm, D_MODEL), lambda i: (i, gate0 + 1)),
            pl.BlockSpec((tm, D_MODEL), row),
            pl.BlockSpec(wa.shape, fixed),
            pl.BlockSpec(wb.shape, fixed),
            pl.BlockSpec(wo.shape, fixed),
        ],
        out_specs=pl.BlockSpec((tm, D_MODEL), row),
        out_shape=jax.ShapeDtypeStruct((t, D_MODEL), F32),
        compiler_params=_params("parallel"),
        name="merge_out",
    )(oa, ob, p16, p16, x2d, wa, wb, wo)


def _ffn_kernel(x_ref, gain_ref, w1_ref, w2_ref, o_ref, h_ref):
    j = pl.program_id(1)

    @pl.when(j == 0)
    def _():
        x = x_ref[...]
        h_ref[...] = (_rms(x) * gain_ref[...]).astype(BF16)
        o_ref[...] = x

    a = jnp.maximum(_dot(h_ref[...], w1_ref[...]), 0.0)
    o_ref[...] += _dot((a * a).astype(BF16), w2_ref[...])


def _ffn(x2d, gain, w1, w2, *, tm, tf):
    t = x2d.shape[0]
    row = lambda i, j: (i, 0)
    return pl.pallas_call(
        _ffn_kernel,
        grid=(t // tm, D_FF // tf),
        in_specs=[
            pl.BlockSpec((tm, D_MODEL), row),
            pl.BlockSpec((1, D_MODEL), lambda i, j: (0, 0)),
            pl.BlockSpec((D_MODEL, tf), lambda i, j: (0, j)),
            pl.BlockSpec((tf, D_MODEL), lambda i, j: (j, 0)),
        ],
        out_specs=pl.BlockSpec((tm, D_MODEL), row),
        out_shape=jax.ShapeDtypeStruct((t, D_MODEL), F32),
        scratch_shapes=[pltpu.VMEM((tm, D_MODEL), BF16)],
        compiler_params=_params("parallel", "arbitrary"),
        name="ffn",
    )(x2d, gain, w1, w2)


def _ple_kernel(x_ref, p_ref, gain_ref, wg_ref, wp_ref, gfin_ref, o_ref):
    x = x_ref[...]
    h = (_rms(x) * gain_ref[...]).astype(BF16)
    gate = jax.nn.sigmoid(_dot(h, wg_ref[...]))
    emb = _dot(p_ref[...].astype(BF16), wp_ref[...])
    o_ref[...] = _rms(x + gate * emb) * gfin_ref[...]


def _ple(x2d, p2d, gain, wg, wp, gfin, *, tm):
    t = x2d.shape[0]
    row = lambda i: (i, 0)
    fixed = lambda i: (0, 0)
    return pl.pallas_call(
        _ple_kernel,
        grid=(t // tm,),
        in_specs=[
            pl.BlockSpec((tm, D_MODEL), row),
            pl.BlockSpec((tm, PLE_DIM), row),
            pl.BlockSpec((1, D_MODEL), fixed),
            pl.BlockSpec(wg.shape, fixed),
            pl.BlockSpec(wp.shape, fixed),
            pl.BlockSpec((1, D_MODEL), fixed),
        ],
        out_specs=pl.BlockSpec((tm, D_MODEL), row),
        out_shape=jax.ShapeDtypeStruct((t, D_MODEL), F32),
        compiler_params=_params("parallel"),
        name="ple_final",
    )(x2d, p2d, gain, wg, wp, gfin)


def kernel(x, p, norm_mix, w_in, lb_logits, hgrn_norm, w_a_out, gmlp_ln_g, gmlp_ln_b,
           w_spatial, b_spatial, w_b_out, w_o, norm_ffn, w_ff1, w_ff2, norm_ple,
           w_ple_gate, w_ple_proj, norm_final):
    batch, seq, _ = x.shape
    assert w_in.shape[0] == 1 and seq % GMLP_CHUNK == 0
    t = batch * seq
    x2d = x.reshape(t, D_MODEL)
    row = lambda a: a.reshape(1, -1)

    p32, p16 = _inproj(x2d, row(norm_mix[0]), w_in[0].astype(BF16), lb_logits,
                       row(gmlp_ln_g[0]), row(gmlp_ln_b[0]), tm=512)
    o_a = _hgrn(p32, p16, row(hgrn_norm[0]), batch, seq)
    bias_full = jnp.repeat(b_spatial[0].T, HEAD_DIM, axis=1)
    o_b = _gmlp(p16, w_spatial[0], bias_full, tm=512)
    x2d = _merge(o_a, o_b, p16, x2d, w_a_out[0].astype(BF16),
                 w_b_out[0].astype(BF16), w_o[0].astype(BF16), tm=512)
    x2d = _ffn(x2d, row(norm_ffn[0]), w_ff1[0].astype(BF16),
               w_ff2[0].astype(BF16), tm=512, tf=1024)
    x2d = _ple(x2d, p[0].reshape(t, PLE_DIM), row(norm_ple[0]),
               w_ple_gate[0].astype(BF16), w_ple_proj[0].astype(BF16),
               row(norm_final), tm=256)
    return x2d.reshape(batch, seq, D_MODEL)
```

```python
import functools

import jax
import jax.numpy as jnp
import numpy as np
from jax import lax
from jax.experimental import pallas as pl
from jax.experimental.pallas import tpu as pltpu

F32 = jnp.float32
BF16 = jnp.bfloat16

D_MODEL = 2048
WIDTH = D_MODEL // 2
HEAD_DIM = 128
HEADS = WIDTH // HEAD_DIM
CHUNK = 64
GMLP_CHUNK = 128
D_FF = 4 * D_MODEL
PLE_DIM = 256
EPS = 1e-6
N_IN = 6 * WIDTH + 2 * D_MODEL

SCAN_LEVELS = 6
VMEM_LIMIT_BYTES = 56 * 1024 * 1024
INPROJ_SUB = 256
SCORES_UNROLL = 8
MIX_UNROLL = 8

SEG_Q, SEG_F, SEG_INP, SEG_G, SEG_U, SEG_V, SEG_GATE = 0, 1, 2, 3, 4, 5, 6
N32_SEGS = 2


def _rms(x):
    return x * lax.rsqrt(jnp.mean(x * x, axis=-1, keepdims=True) + EPS)


def _dot(a, b):
    return jnp.dot(a, b, preferred_element_type=F32)


def _dot_nt(a, b):
    return lax.dot_general(a, b, (((1,), (1,)), ((), ())), preferred_element_type=F32)


def _dot_tn(a, b):
    return lax.dot_general(a, b, (((0,), (0,)), ((), ())), preferred_element_type=F32)


def _gelu(x):
    return x * (lax.erf(x * np.float32(1.0 / np.sqrt(2.0))) + 1.0) * 0.5


def _params(*semantics):
    return pltpu.CompilerParams(
        dimension_semantics=semantics, vmem_limit_bytes=VMEM_LIMIT_BYTES)


def _inproj_kernel(x_ref, gain_ref, w_ref, lbl_ref, lng_ref, lnb_ref, o32_ref, o16_ref, h_ref):
    j = pl.program_id(1)

    @pl.when(j == 0)
    def _():
        h_ref[...] = (_rms(x_ref[...]) * gain_ref[...]).astype(BF16)

    def emit(o_ref, act):
        for n in range(WIDTH // INPROJ_SUB):
            cols = slice(n * INPROJ_SUB, (n + 1) * INPROJ_SUB)
            o_ref[:, cols] = act(_dot(h_ref[...], w_ref[:, cols].astype(BF16)), cols).astype(o_ref.dtype)

    silu = lambda a, cols: a * jax.nn.sigmoid(a)

    @pl.when(j == SEG_Q)
    def _():
        emit(o32_ref, silu)

    @pl.when(j == SEG_F)
    def _():
        logits = lbl_ref[...]
        e = jnp.exp(logits - jnp.max(logits, axis=0, keepdims=True))
        lb = (e / jnp.sum(e, axis=0, keepdims=True))[0:1, :]
        emit(o32_ref, lambda a, cols: jnp.log(
            lb[:, cols] + (1.0 - lb[:, cols]) * jax.nn.sigmoid(a)))

    @pl.when(j == SEG_INP)
    def _():
        emit(o16_ref, lambda a, cols: a)

    @pl.when(j == SEG_G)
    def _():
        emit(o16_ref, silu)

    @pl.when(j == SEG_U)
    def _():
        emit(o16_ref, lambda a, cols: _gelu(a))

    @pl.when(j == SEG_V)
    def _():
        a = _gelu(_dot(h_ref[...], w_ref[...].astype(BF16)))
        mu = jnp.mean(a, axis=-1, keepdims=True)
        c = a - mu
        var = jnp.mean(c * c, axis=-1, keepdims=True)
        o16_ref[...] = (c * lax.rsqrt(var + EPS) * lng_ref[...] + lnb_ref[...]).astype(BF16)

    @pl.when(j >= SEG_GATE)
    def _():
        emit(o16_ref, lambda a, cols: jax.nn.sigmoid(a))


def _inproj(x2d, gain, w, lb_logits, ln_g, ln_b, *, tm):
    t = x2d.shape[0]
    row = lambda i, j: (i, 0)
    fixed = lambda i, j: (0, 0)
    n32 = N32_SEGS * WIDTH
    return pl.pallas_call(
        _inproj_kernel,
        grid=(t // tm, N_IN // WIDTH),
        in_specs=[
            pl.BlockSpec((tm, D_MODEL), row),
            pl.BlockSpec((1, D_MODEL), fixed),
            pl.BlockSpec((D_MODEL, WIDTH), lambda i, j: (0, j)),
            pl.BlockSpec(lb_logits.shape, fixed),
            pl.BlockSpec((1, WIDTH), fixed),
            pl.BlockSpec((1, WIDTH), fixed),
        ],
        out_specs=[
            pl.BlockSpec((tm, WIDTH), lambda i, j: (i, jnp.minimum(j, N32_SEGS - 1))),
            pl.BlockSpec((tm, WIDTH), lambda i, j: (i, jnp.maximum(j - N32_SEGS, 0))),
        ],
        out_shape=[jax.ShapeDtypeStruct((t, n32), F32),
                   jax.ShapeDtypeStruct((t, N_IN - n32), BF16)],
        scratch_shapes=[pltpu.VMEM((tm, D_MODEL), BF16)],
        compiler_params=_params("parallel", "arbitrary"),
        name="inproj",
    )(x2d, gain, w, lb_logits, ln_g, ln_b)


def _pair_masks():
    t = np.arange(CHUNK)[:, None]
    s = np.arange(CHUNK)[None, :]
    masks = []
    for lvl in range(SCAN_LEVELS):
        same = (t >> (lvl + 1)) == (s >> (lvl + 1))
        masks.append(same & (((t >> lvl) & 1) == 1) & (((s >> lvl) & 1) == 0))
    masks.append(t == s)
    return np.stack(masks).astype(np.float32)


def _block_ref_row(p, lvl):
    d = p.shape[-1]
    if lvl >= 2:
        half = 1 << lvl
        p3 = p.reshape(CHUNK // (2 * half), 2 * half, d)
        return jnp.broadcast_to(p3[:, half - 1:half, :], p3.shape).reshape(CHUNK, d)
    sub = lax.broadcasted_iota(jnp.int32, (CHUNK, d), 0)
    prev1 = pltpu.roll(p, 1, axis=0)
    if lvl == 0:
        return jnp.where((sub & 1) == 1, prev1, p)
    pos = sub & 3
    nxt1 = pltpu.roll(p, CHUNK - 1, axis=0)
    prev2 = pltpu.roll(p, 2, axis=0)
    return jnp.where(pos == 0, nxt1, jnp.where(pos == 1, p, jnp.where(pos == 2, prev1, prev2)))


def _hgrn_kernel(n_side, q_ref, lf_ref, v_ref, g_ref, mask_ref, gn_ref, *refs):
    for src_ref, dst_ref in zip(refs[:n_side], refs[n_side + 1:2 * n_side + 1]):
        dst_ref[...] = src_ref[...].astype(BF16)
    o_ref = refs[n_side]
    qs_ref, ks_ref, dec_ref, at_ref, st_ref = refs[2 * n_side + 1:]
    n_chunks = q_ref.shape[0] // CHUNK
    row_id = lax.broadcasted_iota(jnp.int32, (CHUNK, HEAD_DIM), 0)

    def scale(c, carry):
        rows = pl.ds(pl.multiple_of(c * CHUNK, CHUNK), CHUNK)
        q = q_ref[rows, :]
        lf = lf_ref[rows, :]
        k = 1.0 - jnp.exp(lf)
        qs_ref[0, rows, :] = q.astype(BF16)
        ks_ref[0, rows, :] = k.astype(BF16)
        p = lf
        for lvl in range(SCAN_LEVELS):
            upper = ((row_id >> lvl) & 1) == 1
            ref = _block_ref_row(p, lvl)
            w = jnp.exp(jnp.where(upper, p, ref - p))
            qs_ref[1 + lvl, rows, :] = (q * w).astype(BF16)
            if lvl > 0:
                ks_ref[lvl, rows, :] = (k * w).astype(BF16)
            p = p + jnp.where(upper, ref, 0.0)
        last = p[CHUNK - 1:CHUNK, :]
        qs_ref[1 + SCAN_LEVELS, rows, :] = (q * jnp.exp(p)).astype(BF16)
        ks_ref[SCAN_LEVELS, rows, :] = (k * jnp.exp(last - p)).astype(BF16)
        dec_ref[c] = jnp.broadcast_to(jnp.exp(last), (8, HEAD_DIM))
        return carry

    lax.fori_loop(0, n_chunks, scale, 0)

    def scores(c, st):
        rows = pl.ds(pl.multiple_of(c * CHUNK, CHUNK), CHUNK)
        attn = mask_ref[SCAN_LEVELS] * _dot_nt(qs_ref[0, rows, :], ks_ref[0, rows, :])
        for lvl in range(SCAN_LEVELS):
            s = _dot_nt(qs_ref[1 + lvl, rows, :], ks_ref[lvl, rows, :])
            attn = attn + mask_ref[lvl] * s
        at_ref[rows, :] = attn.astype(BF16)
        st_ref[c] = st.astype(BF16)
        return st * dec_ref[c][0:1, :] + _dot_tn(v_ref[rows, :], ks_ref[SCAN_LEVELS, rows, :])

    lax.fori_loop(0, n_chunks, scores, jnp.zeros((HEAD_DIM, HEAD_DIM), F32),
                  unroll=SCORES_UNROLL)

    gn = gn_ref[...]

    def mix(c, carry):
        rows = pl.ds(pl.multiple_of(c * CHUNK, CHUNK), CHUNK)
        o = _dot_nt(qs_ref[1 + SCAN_LEVELS, rows, :], st_ref[c])
        o = o + _dot(at_ref[rows, :], v_ref[rows, :])
        o_ref[rows, :] = (_rms(o) * gn * g_ref[rows, :].astype(F32)).astype(BF16)
        return carry

    lax.fori_loop(0, n_chunks, mix, 0, unroll=MIX_UNROLL)


def _hgrn(p32, p16, gn, side_weights, batch, seq):
    masks = _pair_masks()
    steps = batch * HEADS
    blk = lambda seg: pl.BlockSpec((seq, HEAD_DIM), lambda b, h: (b, seg * HEADS + h))
    side_specs = [pl.BlockSpec((w.shape[0] // steps, w.shape[1]), lambda b, h: (b * HEADS + h, 0))
                  for w in side_weights]
    outs = pl.pallas_call(
        functools.partial(_hgrn_kernel, len(side_weights)),
        grid=(batch, HEADS),
        in_specs=[
            blk(SEG_Q), blk(SEG_F), blk(SEG_INP - N32_SEGS), blk(SEG_G - N32_SEGS),
            pl.BlockSpec(masks.shape, lambda b, h: (0, 0, 0)),
            pl.BlockSpec((1, HEAD_DIM), lambda b, h: (0, 0)),
        ] + side_specs,
        out_specs=[pl.BlockSpec((seq, HEAD_DIM), lambda b, h: (b, h))] + side_specs,
        out_shape=[jax.ShapeDtypeStruct((batch * seq, WIDTH), BF16)]
        + [jax.ShapeDtypeStruct(w.shape, BF16) for w in side_weights],
        scratch_shapes=[
            pltpu.VMEM((SCAN_LEVELS + 2, seq, HEAD_DIM), BF16),
            pltpu.VMEM((SCAN_LEVELS + 1, seq, HEAD_DIM), BF16),
            pltpu.VMEM((seq // CHUNK, 8, HEAD_DIM), F32),
            pltpu.VMEM((seq, CHUNK), BF16),
            pltpu.VMEM((seq // CHUNK, HEAD_DIM, HEAD_DIM), BF16),
        ],
        compiler_params=_params("parallel", "arbitrary"),
        name="hgrn_scan",
    )(p32, p32, p16, p16, jnp.asarray(masks), gn, *side_weights)
    return outs[0], outs[1:]


def _gmlp_kernel(u_ref, v_ref, w_ref, bias_ref, o_ref):
    t = lax.broadcasted_iota(jnp.int32, (GMLP_CHUNK, GMLP_CHUNK), 0)
    s = lax.broadcasted_iota(jnp.int32, (GMLP_CHUNK, GMLP_CHUNK), 1)
    keep = (s // CHUNK) <= (t // CHUNK)
    for h in range(HEADS):
        cols = slice(h * HEAD_DIM, (h + 1) * HEAD_DIM)
        w = jnp.where(keep, w_ref[h], 0.0).astype(BF16)
        bias = bias_ref[:, cols]
        for g in range(u_ref.shape[0] // GMLP_CHUNK):
            rows = slice(g * GMLP_CHUNK, (g + 1) * GMLP_CHUNK)
            sv = _dot(w, v_ref[rows, cols]) + bias
            o_ref[rows, cols] = (u_ref[rows, cols].astype(F32) * sv).astype(BF16)


def _gmlp(p16, w_spatial, bias_full, *, tm):
    t = p16.shape[0]
    return pl.pallas_call(
        _gmlp_kernel,
        grid=(t // tm,),
        in_specs=[
            pl.BlockSpec((tm, WIDTH), lambda i: (i, SEG_U - N32_SEGS)),
            pl.BlockSpec((tm, WIDTH), lambda i: (i, SEG_V - N32_SEGS)),
            pl.BlockSpec(w_spatial.shape, lambda i: (0, 0, 0)),
            pl.BlockSpec(bias_full.shape, lambda i: (0, 0)),
        ],
        out_specs=pl.BlockSpec((tm, WIDTH), lambda i: (i, 0)),
        out_shape=jax.ShapeDtypeStruct((t, WIDTH), BF16),
        compiler_params=_params("parallel"),
        name="gmlp_spatial",
    )(p16, p16, w_spatial, bias_full)


def _merge_kernel(a_ref, b_ref, ga_ref, gb_ref, x_ref, wa_ref, wb_ref, wo_ref, o_ref):
    ya = _dot(a_ref[...], wa_ref[...])
    yb = _dot(b_ref[...], wb_ref[...])
    merged = (ga_ref[...].astype(F32) * ya + gb_ref[...].astype(F32) * yb).astype(BF16)
    o_ref[...] = x_ref[...] + _dot(merged, wo_ref[...])


def _merge(oa, ob, p16, x2d, wa, wb, wo, *, tm):
    t = x2d.shape[0]
    row = lambda i: (i, 0)
    fixed = lambda i: (0, 0)
    gate0 = (SEG_GATE - N32_SEGS) * WIDTH // D_MODEL
    return pl.pallas_call(
        _merge_kernel,
        grid=(t // tm,),
        in_specs=[
            pl.BlockSpec((tm, WIDTH), row),
            pl.BlockSpec((tm, WIDTH), row),
            pl.BlockSpec((tm, D_MODEL), lambda i: (i, gate0)),
            pl.BlockSpec((tm, D_MODEL), lambda i: (i, gate0 + 1)),
            pl.BlockSpec((tm, D_MODEL), row),
            pl.BlockSpec(wa.shape, fixed),
            pl.BlockSpec(wb.shape, fixed),
            pl.BlockSpec(wo.shape, fixed),
        ],
        out_specs=pl.BlockSpec((tm, D_MODEL), row),
        out_shape=jax.ShapeDtypeStruct((t, D_MODEL), F32),
        compiler_params=_params("parallel"),
        name="merge_out",
    )(oa, ob, p16, p16, x2d, wa, wb, wo)


def _ffn_kernel(x_ref, gain_ref, w1_ref, w2_ref, o_ref, h_ref):
    j = pl.program_id(1)

    @pl.when(j == 0)
    def _():
        x = x_ref[...]
        h_ref[...] = (_rms(x) * gain_ref[...]).astype(BF16)
        o_ref[...] = x

    a = jnp.maximum(_dot(h_ref[...], w1_ref[...]), 0.0)
    o_ref[...] += _dot((a * a).astype(BF16), w2_ref[...])


def _ffn(x2d, gain, w1, w2, *, tm, tf):
    t = x2d.shape[0]
    row = lambda i, j: (i, 0)
    return pl.pallas_call(
        _ffn_kernel,
        grid=(t // tm, D_FF // tf),
        in_specs=[
            pl.BlockSpec((tm, D_MODEL), row),
            pl.BlockSpec((1, D_MODEL), lambda i, j: (0, 0)),
            pl.BlockSpec((D_MODEL, tf), lambda i, j: (0, j)),
            pl.BlockSpec((tf, D_MODEL), lambda i, j: (j, 0)),
        ],
        out_specs=pl.BlockSpec((tm, D_MODEL), row),
        out_shape=jax.ShapeDtypeStruct((t, D_MODEL), F32),
        scratch_shapes=[pltpu.VMEM((tm, D_MODEL), BF16)],
        compiler_params=_params("parallel", "arbitrary"),
        name="ffn",
    )(x2d, gain, w1, w2)


def _ple_kernel(x_ref, p_ref, gain_ref, wg_ref, wp_ref, gfin_ref, o_ref):
    x = x_ref[...]
    h = (_rms(x) * gain_ref[...]).astype(BF16)
    gate = jax.nn.sigmoid(_dot(h, wg_ref[...]))
    emb = _dot(p_ref[...].astype(BF16), wp_ref[...])
    o_ref[...] = _rms(x + gate * emb) * gfin_ref[...]


def _ple(x2d, p2d, gain, wg, wp, gfin, *, tm):
    t = x2d.shape[0]
    row = lambda i: (i, 0)
    fixed = lambda i: (0, 0)
    return pl.pallas_call(
        _ple_kernel,
        grid=(t // tm,),
        in_specs=[
            pl.BlockSpec((tm, D_MODEL), row),
            pl.BlockSpec((tm, PLE_DIM), row),
            pl.BlockSpec((1, D_MODEL), fixed),
            pl.BlockSpec(wg.shape, fixed),
            pl.BlockSpec(wp.shape, fixed),
            pl.BlockSpec((1, D_MODEL), fixed),
        ],
        out_specs=pl.BlockSpec((tm, D_MODEL), row),
        out_shape=jax.ShapeDtypeStruct((t, D_MODEL), F32),
        compiler_params=_params("parallel"),
        name="ple_final",
    )(x2d, p2d, gain, wg, wp, gfin)


def kernel(x, p, norm_mix, w_in, lb_logits, hgrn_norm, w_a_out, gmlp_ln_g, gmlp_ln_b,
           w_spatial, b_spatial, w_b_out, w_o, norm_ffn, w_ff1, w_ff2, norm_ple,
           w_ple_gate, w_ple_proj, norm_final):
    batch, seq, _ = x.shape
    assert w_in.shape[0] == 1 and seq % GMLP_CHUNK == 0
    t = batch * seq
    x2d = x.reshape(t, D_MODEL)
    row = lambda a: a.reshape(1, -1)

    p32, p16 = _inproj(x2d, row(norm_mix[0]), w_in[0], lb_logits,
                       row(gmlp_ln_g[0]), row(gmlp_ln_b[0]), tm=1024)
    o_a, (wa, wb, wo, w1, w2, wg) = _hgrn(
        p32, p16, row(hgrn_norm[0]),
        [w_a_out[0], w_b_out[0], w_o[0], w_ff1[0], w_ff2[0], w_ple_gate[0]], batch, seq)
    bias_full = jnp.repeat(b_spatial[0].T, HEAD_DIM, axis=1)
    o_b = _gmlp(p16, w_spatial[0], bias_full, tm=512)
    x2d = _merge(o_a, o_b, p16, x2d, wa, wb, wo, tm=512)
    x2d = _ffn(x2d, row(norm_ffn[0]), w1, w2, tm=512, tf=1024)
    x2d = _ple(x2d, p[0].reshape(t, PLE_DIM), row(norm_ple[0]), wg,
               w_ple_proj[0].astype(BF16), row(norm_final), tm=256)
    return x2d.reshape(batch, seq, D_MODEL)
```

```python
import functools

import jax
import jax.numpy as jnp
import numpy as np
from jax import lax
from jax.experimental import pallas as pl
from jax.experimental.pallas import tpu as pltpu

F32 = jnp.float32
BF16 = jnp.bfloat16

D_MODEL = 2048
WIDTH = D_MODEL // 2
HEAD_DIM = 128
HEADS = WIDTH // HEAD_DIM
CHUNK = 64
GMLP_CHUNK = 128
D_FF = 4 * D_MODEL
PLE_DIM = 256
EPS = 1e-6
N_IN = 6 * WIDTH + 2 * D_MODEL

SCAN_LEVELS = 6
VMEM_LIMIT_BYTES = 56 * 1024 * 1024
INPROJ_SUB = 256
SCAN_GROUP = 4

SEG_Q, SEG_F, SEG_INP, SEG_G, SEG_U, SEG_V, SEG_GATE = 0, 1, 2, 3, 4, 5, 6
N32_SEGS = 2


def _rms(x):
    return x * lax.rsqrt(jnp.mean(x * x, axis=-1, keepdims=True) + EPS)


def _dot(a, b):
    return jnp.dot(a, b, preferred_element_type=F32)


def _dot_nt(a, b):
    return lax.dot_general(a, b, (((1,), (1,)), ((), ())), preferred_element_type=F32)


def _dot_tn(a, b):
    return lax.dot_general(a, b, (((0,), (0,)), ((), ())), preferred_element_type=F32)


def _gelu(x):
    return x * (lax.erf(x * np.float32(1.0 / np.sqrt(2.0))) + 1.0) * 0.5


def _params(*semantics):
    return pltpu.CompilerParams(
        dimension_semantics=semantics, vmem_limit_bytes=VMEM_LIMIT_BYTES)


def _inproj_kernel(x_ref, gain_ref, w_ref, lbl_ref, lng_ref, lnb_ref, o32_ref, o16_ref, h_ref):
    j = pl.program_id(1)

    @pl.when(j == 0)
    def _():
        h_ref[...] = (_rms(x_ref[...]) * gain_ref[...]).astype(BF16)

    def emit(o_ref, act):
        for n in range(WIDTH // INPROJ_SUB):
            cols = slice(n * INPROJ_SUB, (n + 1) * INPROJ_SUB)
            o_ref[:, cols] = act(_dot(h_ref[...], w_ref[:, cols].astype(BF16)), cols).astype(o_ref.dtype)

    silu = lambda a, cols: a * jax.nn.sigmoid(a)

    @pl.when(j == SEG_Q)
    def _():
        emit(o32_ref, silu)

    @pl.when(j == SEG_F)
    def _():
        logits = lbl_ref[...]
        e = jnp.exp(logits - jnp.max(logits, axis=0, keepdims=True))
        lb = (e / jnp.sum(e, axis=0, keepdims=True))[0:1, :]
        emit(o32_ref, lambda a, cols: jnp.log(
            lb[:, cols] + (1.0 - lb[:, cols]) * jax.nn.sigmoid(a)))

    @pl.when(j == SEG_INP)
    def _():
        emit(o16_ref, lambda a, cols: a)

    @pl.when(j == SEG_G)
    def _():
        emit(o16_ref, silu)

    @pl.when(j == SEG_U)
    def _():
        emit(o16_ref, lambda a, cols: _gelu(a))

    @pl.when(j == SEG_V)
    def _():
        a = _gelu(_dot(h_ref[...], w_ref[...].astype(BF16)))
        mu = jnp.mean(a, axis=-1, keepdims=True)
        c = a - mu
        var = jnp.mean(c * c, axis=-1, keepdims=True)
        o16_ref[...] = (c * lax.rsqrt(var + EPS) * lng_ref[...] + lnb_ref[...]).astype(BF16)

    @pl.when(j >= SEG_GATE)
    def _():
        emit(o16_ref, lambda a, cols: jax.nn.sigmoid(a))


def _inproj(x2d, gain, w, lb_logits, ln_g, ln_b, *, tm):
    t = x2d.shape[0]
    row = lambda i, j: (i, 0)
    fixed = lambda i, j: (0, 0)
    n32 = N32_SEGS * WIDTH
    return pl.pallas_call(
        _inproj_kernel,
        grid=(t // tm, N_IN // WIDTH),
        in_specs=[
            pl.BlockSpec((tm, D_MODEL), row),
            pl.BlockSpec((1, D_MODEL), fixed),
            pl.BlockSpec((D_MODEL, WIDTH), lambda i, j: (0, j)),
            pl.BlockSpec(lb_logits.shape, fixed),
            pl.BlockSpec((1, WIDTH), fixed),
            pl.BlockSpec((1, WIDTH), fixed),
        ],
        out_specs=[
            pl.BlockSpec((tm, WIDTH), lambda i, j: (i, jnp.minimum(j, N32_SEGS - 1))),
            pl.BlockSpec((tm, WIDTH), lambda i, j: (i, jnp.maximum(j - N32_SEGS, 0))),
        ],
        out_shape=[jax.ShapeDtypeStruct((t, n32), F32),
                   jax.ShapeDtypeStruct((t, N_IN - n32), BF16)],
        scratch_shapes=[pltpu.VMEM((tm, D_MODEL), BF16)],
        compiler_params=_params("parallel", "arbitrary"),
        name="inproj",
    )(x2d, gain, w, lb_logits, ln_g, ln_b)


def _pair_masks():
    t = np.arange(CHUNK)[:, None]
    s = np.arange(CHUNK)[None, :]
    masks = []
    for lvl in range(SCAN_LEVELS):
        same = (t >> (lvl + 1)) == (s >> (lvl + 1))
        masks.append(same & (((t >> lvl) & 1) == 1) & (((s >> lvl) & 1) == 0))
    masks.append(t == s)
    return np.stack(masks).astype(np.float32)


def _block_ref_row(p, lvl):
    d = p.shape[-1]
    if lvl >= 2:
        half = 1 << lvl
        p3 = p.reshape(CHUNK // (2 * half), 2 * half, d)
        return jnp.broadcast_to(p3[:, half - 1:half, :], p3.shape).reshape(CHUNK, d)
    sub = lax.broadcasted_iota(jnp.int32, (CHUNK, d), 0)
    prev1 = pltpu.roll(p, 1, axis=0)
    if lvl == 0:
        return jnp.where((sub & 1) == 1, prev1, p)
    pos = sub & 3
    nxt1 = pltpu.roll(p, CHUNK - 1, axis=0)
    prev2 = pltpu.roll(p, 2, axis=0)
    return jnp.where(pos == 0, nxt1, jnp.where(pos == 1, p, jnp.where(pos == 2, prev1, prev2)))


def _hgrn_kernel(n_side, q_ref, lf_ref, v_ref, g_ref, mask_ref, gn_ref, *refs):
    for src_ref, dst_ref in zip(refs[:n_side], refs[n_side + 1:2 * n_side + 1]):
        dst_ref[...] = src_ref[...].astype(BF16)
    o_ref = refs[n_side]
    qs_ref, ks_ref, dec_ref, at_ref, st_ref = refs[2 * n_side + 1:]
    n_groups = q_ref.shape[0] // (CHUNK * SCAN_GROUP)
    row_id = lax.broadcasted_iota(jnp.int32, (CHUNK, HEAD_DIM), 0)
    gn = gn_ref[...]

    def chunk_rows(c):
        start = c * CHUNK
        return pl.ds(start if isinstance(c, int) else pl.multiple_of(start, CHUNK), CHUNK)

    def scale(c):
        rows = chunk_rows(c)
        q = q_ref[rows, :]
        lf = lf_ref[rows, :]
        k = 1.0 - jnp.exp(lf)
        qs_ref[0, rows, :] = q.astype(BF16)
        ks_ref[0, rows, :] = k.astype(BF16)
        p = lf
        for lvl in range(SCAN_LEVELS):
            upper = ((row_id >> lvl) & 1) == 1
            ref = _block_ref_row(p, lvl)
            w = jnp.exp(jnp.where(upper, p, ref - p))
            qs_ref[1 + lvl, rows, :] = (q * w).astype(BF16)
            if lvl > 0:
                ks_ref[lvl, rows, :] = (k * w).astype(BF16)
            p = p + jnp.where(upper, ref, 0.0)
        last = p[CHUNK - 1:CHUNK, :]
        qs_ref[1 + SCAN_LEVELS, rows, :] = (q * jnp.exp(p)).astype(BF16)
        ks_ref[SCAN_LEVELS, rows, :] = (k * jnp.exp(last - p)).astype(BF16)
        dec_ref[c] = jnp.broadcast_to(jnp.exp(last), (8, HEAD_DIM))

    def scores(c, st):
        rows = chunk_rows(c)
        attn = mask_ref[SCAN_LEVELS] * _dot_nt(qs_ref[0, rows, :], ks_ref[0, rows, :])
        for lvl in range(SCAN_LEVELS):
            s = _dot_nt(qs_ref[1 + lvl, rows, :], ks_ref[lvl, rows, :])
            attn = attn + mask_ref[lvl] * s
        at_ref[rows, :] = attn.astype(BF16)
        st_ref[c] = st.astype(BF16)
        return st * dec_ref[c][0:1, :] + _dot_tn(v_ref[rows, :], ks_ref[SCAN_LEVELS, rows, :])

    def output(c):
        rows = chunk_rows(c)
        o = _dot_nt(qs_ref[1 + SCAN_LEVELS, rows, :], st_ref[c])
        o = o + _dot(at_ref[rows, :], v_ref[rows, :])
        o_ref[rows, :] = (_rms(o) * gn * g_ref[rows, :].astype(F32)).astype(BF16)

    def stage(group, st, do_output, do_scores, do_scale):
        for i in range(SCAN_GROUP):
            if do_output:
                output((group - 2) * SCAN_GROUP + i)
        for i in range(SCAN_GROUP):
            if do_scores:
                st = scores((group - 1) * SCAN_GROUP + i, st)
        for i in range(SCAN_GROUP):
            if do_scale:
                scale(group * SCAN_GROUP + i)
        return st

    st = jnp.zeros((HEAD_DIM, HEAD_DIM), F32)
    st = stage(0, st, False, False, True)
    st = stage(1, st, False, True, True)
    st = lax.fori_loop(2, n_groups, lambda g, s: stage(g, s, True, True, True), st)
    st = stage(n_groups, st, True, True, False)
    stage(n_groups + 1, st, True, False, False)


def _hgrn(p32, p16, gn, side_weights, batch, seq):
    masks = _pair_masks()
    steps = batch * HEADS
    blk = lambda seg: pl.BlockSpec((seq, HEAD_DIM), lambda b, h: (b, seg * HEADS + h))
    side_specs = [pl.BlockSpec((w.shape[0] // steps, w.shape[1]), lambda b, h: (b * HEADS + h, 0))
                  for w in side_weights]
    outs = pl.pallas_call(
        functools.partial(_hgrn_kernel, len(side_weights)),
        grid=(batch, HEADS),
        in_specs=[
            blk(SEG_Q), blk(SEG_F), blk(SEG_INP - N32_SEGS), blk(SEG_G - N32_SEGS),
            pl.BlockSpec(masks.shape, lambda b, h: (0, 0, 0)),
            pl.BlockSpec((1, HEAD_DIM), lambda b, h: (0, 0)),
        ] + side_specs,
        out_specs=[pl.BlockSpec((seq, HEAD_DIM), lambda b, h: (b, h))] + side_specs,
        out_shape=[jax.ShapeDtypeStruct((batch * seq, WIDTH), BF16)]
        + [jax.ShapeDtypeStruct(w.shape, BF16) for w in side_weights],
        scratch_shapes=[
            pltpu.VMEM((SCAN_LEVELS + 2, seq, HEAD_DIM), BF16),
            pltpu.VMEM((SCAN_LEVELS + 1, seq, HEAD_DIM), BF16),
            pltpu.VMEM((seq // CHUNK, 8, HEAD_DIM), F32),
            pltpu.VMEM((seq, CHUNK), BF16),
            pltpu.VMEM((seq // CHUNK, HEAD_DIM, HEAD_DIM), BF16),
        ],
        compiler_params=_params("parallel", "arbitrary"),
        name="hgrn_scan",
    )(p32, p32, p16, p16, jnp.asarray(masks), gn, *side_weights)
    return outs[0], outs[1:]


def _gmlp_kernel(u_ref, v_ref, w_ref, bias_ref, o_ref):
    t = lax.broadcasted_iota(jnp.int32, (GMLP_CHUNK, GMLP_CHUNK), 0)
    s = lax.broadcasted_iota(jnp.int32, (GMLP_CHUNK, GMLP_CHUNK), 1)
    keep = (s // CHUNK) <= (t // CHUNK)
    for h in range(HEADS):
        cols = slice(h * HEAD_DIM, (h + 1) * HEAD_DIM)
        w = jnp.where(keep, w_ref[h], 0.0).astype(BF16)
        bias = bias_ref[:, cols]
        for g in range(u_ref.shape[0] // GMLP_CHUNK):
            rows = slice(g * GMLP_CHUNK, (g + 1) * GMLP_CHUNK)
            sv = _dot(w, v_ref[rows, cols]) + bias
            o_ref[rows, cols] = (u_ref[rows, cols].astype(F32) * sv).astype(BF16)


def _gmlp(p16, w_spatial, bias_full, *, tm):
    t = p16.shape[0]
    return pl.pallas_call(
        _gmlp_kernel,
        grid=(t // tm,),
        in_specs=[
            pl.BlockSpec((tm, WIDTH), lambda i: (i, SEG_U - N32_SEGS)),
            pl.BlockSpec((tm, WIDTH), lambda i: (i, SEG_V - N32_SEGS)),
            pl.BlockSpec(w_spatial.shape, lambda i: (0, 0, 0)),
            pl.BlockSpec(bias_full.shape, lambda i: (0, 0)),
        ],
        out_specs=pl.BlockSpec((tm, WIDTH), lambda i: (i, 0)),
        out_shape=jax.ShapeDtypeStruct((t, WIDTH), BF16),
        compiler_params=_params("parallel"),
        name="gmlp_spatial",
    )(p16, p16, w_spatial, bias_full)


def _merge_kernel(a_ref, b_ref, ga_ref, gb_ref, x_ref, wa_ref, wb_ref, wo_ref, o_ref):
    ya = _dot(a_ref[...], wa_ref[...])
    yb = _dot(b_ref[...], wb_ref[...])
    merged = (ga_ref[...].astype(F32) * ya + gb_ref[...].astype(F32) * yb).astype(BF16)
    o_ref[...] = x_ref[...] + _dot(merged, wo_ref[...])


def _merge(oa, ob, p16, x2d, wa, wb, wo, *, tm):
    t = x2d.shape[0]
    row = lambda i: (i, 0)
    fixed = lambda i: (0, 0)
    gate0 = (SEG_GATE - N32_SEGS) * WIDTH // D_MODEL
    return pl.pallas_call(
        _merge_kernel,
        grid=(t // tm,),
        in_specs=[
            pl.BlockSpec((tm, WIDTH), row),
            pl.BlockSpec((tm, WIDTH), row),
            pl.BlockSpec((tm, D_MODEL), lambda i: (i, gate0)),
            pl.BlockSpec((tm, D_MODEL), lambda i: (i, gate0 + 1)),
            pl.BlockSpec((tm, D_MODEL), row),
            pl.BlockSpec(wa.shape, fixed),
            pl.BlockSpec(wb.shape, fixed),
            pl.BlockSpec(wo.shape, fixed),
        ],
        out_specs=pl.BlockSpec((tm, D_MODEL), row),
        out_shape=jax.ShapeDtypeStruct((t, D_MODEL), F32),
        compiler_params=_params("parallel"),
        name="merge_out",
    )(oa, ob, p16, p16, x2d, wa, wb, wo)


def _ffn_kernel(x_ref, gain_ref, w1_ref, w2_ref, o_ref, h_ref):
    j = pl.program_id(1)

    @pl.when(j == 0)
    def _():
        x = x_ref[...]
        h_ref[...] = (_rms(x) * gain_ref[...]).astype(BF16)
        o_ref[...] = x

    a = jnp.maximum(_dot(h_ref[...], w1_ref[...]), 0.0)
    o_ref[...] += _dot((a * a).astype(BF16), w2_ref[...])


def _ffn(x2d, gain, w1, w2, *, tm, tf):
    t = x2d.shape[0]
    row = lambda i, j: (i, 0)
    return pl.pallas_call(
        _ffn_kernel,
        grid=(t // tm, D_FF // tf),
        in_specs=[
            pl.BlockSpec((tm, D_MODEL), row),
            pl.BlockSpec((1, D_MODEL), lambda i, j: (0, 0)),
            pl.BlockSpec((D_MODEL, tf), lambda i, j: (0, j)),
            pl.BlockSpec((tf, D_MODEL), lambda i, j: (j, 0)),
        ],
        out_specs=pl.BlockSpec((tm, D_MODEL), row),
        out_shape=jax.ShapeDtypeStruct((t, D_MODEL), F32),
        scratch_shapes=[pltpu.VMEM((tm, D_MODEL), BF16)],
        compiler_params=_params("parallel", "arbitrary"),
        name="ffn",
    )(x2d, gain, w1, w2)


def _ple_kernel(x_ref, p_ref, gain_ref, wg_ref, wp_ref, gfin_ref, o_ref):
    x = x_ref[...]
    h = (_rms(x) * gain_ref[...]).astype(BF16)
    gate = jax.nn.sigmoid(_dot(h, wg_ref[...]))
    emb = _dot(p_ref[...].astype(BF16), wp_ref[...])
    o_ref[...] = _rms(x + gate * emb) * gfin_ref[...]


def _ple(x2d, p2d, gain, wg, wp, gfin, *, tm):
    t = x2d.shape[0]
    row = lambda i: (i, 0)
    fixed = lambda i: (0, 0)
    return pl.pallas_call(
        _ple_kernel,
        grid=(t // tm,),
        in_specs=[
            pl.BlockSpec((tm, D_MODEL), row),
            pl.BlockSpec((tm, PLE_DIM), row),
            pl.BlockSpec((1, D_MODEL), fixed),
            pl.BlockSpec(wg.shape, fixed),
            pl.BlockSpec(wp.shape, fixed),
            pl.BlockSpec((1, D_MODEL), fixed),
        ],
        out_specs=pl.BlockSpec((tm, D_MODEL), row),
        out_shape=jax.ShapeDtypeStruct((t, D_MODEL), F32),
        compiler_params=_params("parallel"),
        name="ple_final",
    )(x2d, p2d, gain, wg, wp, gfin)


def kernel(x, p, norm_mix, w_in, lb_logits, hgrn_norm, w_a_out, gmlp_ln_g, gmlp_ln_b,
           w_spatial, b_spatial, w_b_out, w_o, norm_ffn, w_ff1, w_ff2, norm_ple,
           w_ple_gate, w_ple_proj, norm_final):
    batch, seq, _ = x.shape
    assert w_in.shape[0] == 1 and seq % GMLP_CHUNK == 0
    t = batch * seq
    x2d = x.reshape(t, D_MODEL)
    row = lambda a: a.reshape(1, -1)

    p32, p16 = _inproj(x2d, row(norm_mix[0]), w_in[0], lb_logits,
                       row(gmlp_ln_g[0]), row(gmlp_ln_b[0]), tm=1024)
    o_a, (wa, wb, wo, w1, w2, wg) = _hgrn(
        p32, p16, row(hgrn_norm[0]),
        [w_a_out[0], w_b_out[0], w_o[0], w_ff1[0], w_ff2[0], w_ple_gate[0]], batch, seq)
    bias_full = jnp.repeat(b_spatial[0].T, HEAD_DIM, axis=1)
    o_b = _gmlp(p16, w_spatial[0], bias_full, tm=512)
    x2d = _merge(o_a, o_b, p16, x2d, wa, wb, wo, tm=512)
    x2d = _ffn(x2d, row(norm_ffn[0]), w1, w2, tm=512, tf=1024)
    x2d = _ple(x2d, p[0].reshape(t, PLE_DIM), row(norm_ple[0]), wg,
               w_ple_proj[0].astype(BF16), row(norm_final), tm=256)
    return x2d.reshape(batch, seq, D_MODEL)
```

```python
import functools

import jax
import jax.numpy as jnp
import numpy as np
from jax import lax
from jax.experimental import pallas as pl
from jax.experimental.pallas import tpu as pltpu

F32 = jnp.float32
BF16 = jnp.bfloat16

D_MODEL = 2048
WIDTH = D_MODEL // 2
HEAD_DIM = 128
HEADS = WIDTH // HEAD_DIM
CHUNK = 64
GMLP_CHUNK = 128
D_FF = 4 * D_MODEL
PLE_DIM = 256
EPS = 1e-6
N_IN = 6 * WIDTH + 2 * D_MODEL

SCAN_LEVELS = 6
VMEM_LIMIT_BYTES = 56 * 1024 * 1024
MXU_COLS = 256
SCAN_GROUP = 4

SEG_Q, SEG_F, SEG_INP, SEG_G, SEG_U, SEG_V, SEG_GATE = 0, 1, 2, 3, 4, 5, 6
N32_SEGS = 2


def _rms(x):
    return x * lax.rsqrt(jnp.mean(x * x, axis=-1, keepdims=True) + EPS)


def _dot(a, b):
    return jnp.dot(a, b, preferred_element_type=F32)


def _dot_nt(a, b):
    return lax.dot_general(a, b, (((1,), (1,)), ((), ())), preferred_element_type=F32)


def _dot_tn(a, b):
    return lax.dot_general(a, b, (((0,), (0,)), ((), ())), preferred_element_type=F32)


def _gelu(x):
    return x * (lax.erf(x * np.float32(1.0 / np.sqrt(2.0))) + 1.0) * 0.5


def _sigmoid(x):
    return 0.5 * jnp.tanh(0.5 * x) + 0.5


def _params(*semantics):
    return pltpu.CompilerParams(
        dimension_semantics=semantics, vmem_limit_bytes=VMEM_LIMIT_BYTES)


def _inproj_kernel(x_ref, gain_ref, w_ref, lbl_ref, lng_ref, lnb_ref, o32_ref, o16_ref, h_ref):
    j = pl.program_id(1)

    @pl.when(j == 0)
    def _():
        h_ref[...] = (_rms(x_ref[...]) * gain_ref[...]).astype(BF16)

    def emit(o_ref, act):
        for n in range(WIDTH // MXU_COLS):
            cols = slice(n * MXU_COLS, (n + 1) * MXU_COLS)
            o_ref[:, cols] = act(_dot(h_ref[...], w_ref[:, cols].astype(BF16)), cols).astype(o_ref.dtype)

    silu = lambda a, cols: a * _sigmoid(a)

    @pl.when(j == SEG_Q)
    def _():
        emit(o32_ref, silu)

    @pl.when(j == SEG_F)
    def _():
        logits = lbl_ref[...]
        e = jnp.exp(logits - jnp.max(logits, axis=0, keepdims=True))
        lb = (e / jnp.sum(e, axis=0, keepdims=True))[0:1, :]
        emit(o32_ref, lambda a, cols: jnp.log(
            lb[:, cols] + (1.0 - lb[:, cols]) * jax.nn.sigmoid(a)))

    @pl.when(j == SEG_INP)
    def _():
        emit(o16_ref, lambda a, cols: a)

    @pl.when(j == SEG_G)
    def _():
        emit(o16_ref, silu)

    @pl.when(j == SEG_U)
    def _():
        emit(o16_ref, lambda a, cols: _gelu(a))

    @pl.when(j == SEG_V)
    def _():
        a = _gelu(_dot(h_ref[...], w_ref[...].astype(BF16)))
        mu = jnp.mean(a, axis=-1, keepdims=True)
        c = a - mu
        var = jnp.mean(c * c, axis=-1, keepdims=True)
        o16_ref[...] = (c * lax.rsqrt(var + EPS) * lng_ref[...] + lnb_ref[...]).astype(BF16)

    @pl.when(j >= SEG_GATE)
    def _():
        emit(o16_ref, lambda a, cols: _sigmoid(a))


def _inproj(x2d, gain, w, lb_logits, ln_g, ln_b, *, tm):
    t = x2d.shape[0]
    row = lambda i, j: (i, 0)
    fixed = lambda i, j: (0, 0)
    n32 = N32_SEGS * WIDTH
    return pl.pallas_call(
        _inproj_kernel,
        grid=(t // tm, N_IN // WIDTH),
        in_specs=[
            pl.BlockSpec((tm, D_MODEL), row),
            pl.BlockSpec((1, D_MODEL), fixed),
            pl.BlockSpec((D_MODEL, WIDTH), lambda i, j: (0, j)),
            pl.BlockSpec(lb_logits.shape, fixed),
            pl.BlockSpec((1, WIDTH), fixed),
            pl.BlockSpec((1, WIDTH), fixed),
        ],
        out_specs=[
            pl.BlockSpec((tm, WIDTH), lambda i, j: (i, jnp.minimum(j, N32_SEGS - 1))),
            pl.BlockSpec((tm, WIDTH), lambda i, j: (i, jnp.maximum(j - N32_SEGS, 0))),
        ],
        out_shape=[jax.ShapeDtypeStruct((t, n32), F32),
                   jax.ShapeDtypeStruct((t, N_IN - n32), BF16)],
        scratch_shapes=[pltpu.VMEM((tm, D_MODEL), BF16)],
        compiler_params=_params("parallel", "arbitrary"),
        name="inproj",
    )(x2d, gain, w, lb_logits, ln_g, ln_b)


def _pair_masks():
    t = np.arange(CHUNK)[:, None]
    s = np.arange(CHUNK)[None, :]
    masks = []
    for lvl in range(SCAN_LEVELS):
        same = (t >> (lvl + 1)) == (s >> (lvl + 1))
        masks.append(same & (((t >> lvl) & 1) == 1) & (((s >> lvl) & 1) == 0))
    masks.append(t == s)
    return np.stack(masks).astype(np.float32)


def _block_ref_row(p, lvl):
    d = p.shape[-1]
    if lvl >= 2:
        half = 1 << lvl
        p3 = p.reshape(CHUNK // (2 * half), 2 * half, d)
        return jnp.broadcast_to(p3[:, half - 1:half, :], p3.shape).reshape(CHUNK, d)
    sub = lax.broadcasted_iota(jnp.int32, (CHUNK, d), 0)
    prev1 = pltpu.roll(p, 1, axis=0)
    if lvl == 0:
        return jnp.where((sub & 1) == 1, prev1, p)
    pos = sub & 3
    nxt1 = pltpu.roll(p, CHUNK - 1, axis=0)
    prev2 = pltpu.roll(p, 2, axis=0)
    return jnp.where(pos == 0, nxt1, jnp.where(pos == 1, p, jnp.where(pos == 2, prev1, prev2)))


def _hgrn_kernel(n_side, q_ref, lf_ref, v_ref, g_ref, mask_ref, gn_ref, *refs):
    for src_ref, dst_ref in zip(refs[:n_side], refs[n_side + 1:2 * n_side + 1]):
        dst_ref[...] = src_ref[...].astype(BF16)
    o_ref = refs[n_side]
    qs_ref, ks_ref, dec_ref, at_ref, st_ref = refs[2 * n_side + 1:]
    n_groups = q_ref.shape[0] // (CHUNK * SCAN_GROUP)
    row_id = lax.broadcasted_iota(jnp.int32, (CHUNK, HEAD_DIM), 0)
    gn = gn_ref[...]

    def chunk_rows(c):
        start = c * CHUNK
        return pl.ds(start if isinstance(c, int) else pl.multiple_of(start, CHUNK), CHUNK)

    def scale(c):
        rows = chunk_rows(c)
        q = q_ref[rows, :]
        lf = lf_ref[rows, :]
        k = 1.0 - jnp.exp(lf)
        qs_ref[0, rows, :] = q.astype(BF16)
        ks_ref[0, rows, :] = k.astype(BF16)
        p = lf
        for lvl in range(SCAN_LEVELS):
            upper = ((row_id >> lvl) & 1) == 1
            ref = _block_ref_row(p, lvl)
            w = jnp.exp(jnp.where(upper, p, ref - p))
            qs_ref[1 + lvl, rows, :] = (q * w).astype(BF16)
            if lvl > 0:
                ks_ref[lvl, rows, :] = (k * w).astype(BF16)
            p = p + jnp.where(upper, ref, 0.0)
        last = p[CHUNK - 1:CHUNK, :]
        qs_ref[1 + SCAN_LEVELS, rows, :] = (q * jnp.exp(p)).astype(BF16)
        ks_ref[SCAN_LEVELS, rows, :] = (k * jnp.exp(last - p)).astype(BF16)
        dec_ref[c] = jnp.broadcast_to(jnp.exp(last), (8, HEAD_DIM))

    def scores(c, st):
        rows = chunk_rows(c)
        attn = mask_ref[SCAN_LEVELS] * _dot_nt(qs_ref[0, rows, :], ks_ref[0, rows, :])
        for lvl in range(SCAN_LEVELS):
            s = _dot_nt(qs_ref[1 + lvl, rows, :], ks_ref[lvl, rows, :])
            attn = attn + mask_ref[lvl] * s
        at_ref[rows, :] = attn.astype(BF16)
        st_ref[c] = st.astype(BF16)
        return st * dec_ref[c][0:1, :] + _dot_tn(v_ref[rows, :], ks_ref[SCAN_LEVELS, rows, :])

    def output(c):
        rows = chunk_rows(c)
        o = _dot_nt(qs_ref[1 + SCAN_LEVELS, rows, :], st_ref[c])
        o = o + _dot(at_ref[rows, :], v_ref[rows, :])
        o_ref[rows, :] = (_rms(o) * gn * g_ref[rows, :].astype(F32)).astype(BF16)

    def stage(group, st, do_output, do_scores, do_scale):
        for i in range(SCAN_GROUP):
            if do_output:
                output((group - 2) * SCAN_GROUP + i)
        for i in range(SCAN_GROUP):
            if do_scores:
                st = scores((group - 1) * SCAN_GROUP + i, st)
        for i in range(SCAN_GROUP):
            if do_scale:
                scale(group * SCAN_GROUP + i)
        return st

    st = jnp.zeros((HEAD_DIM, HEAD_DIM), F32)
    st = stage(0, st, False, False, True)
    st = stage(1, st, False, True, True)
    st = lax.fori_loop(2, n_groups, lambda g, s: stage(g, s, True, True, True), st)
    st = stage(n_groups, st, True, True, False)
    stage(n_groups + 1, st, True, False, False)


def _hgrn(p32, p16, gn, side_weights, batch, seq):
    masks = _pair_masks()
    steps = batch * HEADS
    blk = lambda seg: pl.BlockSpec((seq, HEAD_DIM), lambda b, h: (b, seg * HEADS + h))
    side_specs = [pl.BlockSpec((w.shape[0] // steps, w.shape[1]), lambda b, h: (b * HEADS + h, 0))
                  for w in side_weights]
    outs = pl.pallas_call(
        functools.partial(_hgrn_kernel, len(side_weights)),
        grid=(batch, HEADS),
        in_specs=[
            blk(SEG_Q), blk(SEG_F), blk(SEG_INP - N32_SEGS), blk(SEG_G - N32_SEGS),
            pl.BlockSpec(masks.shape, lambda b, h: (0, 0, 0)),
            pl.BlockSpec((1, HEAD_DIM), lambda b, h: (0, 0)),
        ] + side_specs,
        out_specs=[pl.BlockSpec((seq, HEAD_DIM), lambda b, h: (b, h))] + side_specs,
        out_shape=[jax.ShapeDtypeStruct((batch * seq, WIDTH), BF16)]
        + [jax.ShapeDtypeStruct(w.shape, BF16) for w in side_weights],
        scratch_shapes=[
            pltpu.VMEM((SCAN_LEVELS + 2, seq, HEAD_DIM), BF16),
            pltpu.VMEM((SCAN_LEVELS + 1, seq, HEAD_DIM), BF16),
            pltpu.VMEM((seq // CHUNK, 8, HEAD_DIM), F32),
            pltpu.VMEM((seq, CHUNK), BF16),
            pltpu.VMEM((seq // CHUNK, HEAD_DIM, HEAD_DIM), BF16),
        ],
        compiler_params=_params("parallel", "arbitrary"),
        name="hgrn_scan",
    )(p32, p32, p16, p16, jnp.asarray(masks), gn, *side_weights)
    return outs[0], outs[1:]


def _gmlp_kernel(u_ref, v_ref, w_ref, bias_ref, o_ref):
    t = lax.broadcasted_iota(jnp.int32, (GMLP_CHUNK, GMLP_CHUNK), 0)
    s = lax.broadcasted_iota(jnp.int32, (GMLP_CHUNK, GMLP_CHUNK), 1)
    keep = (s // CHUNK) <= (t // CHUNK)
    for h in range(HEADS):
        cols = slice(h * HEAD_DIM, (h + 1) * HEAD_DIM)
        w = jnp.where(keep, w_ref[h], 0.0).astype(BF16)
        bias = bias_ref[:, cols]
        for g in range(u_ref.shape[0] // GMLP_CHUNK):
            rows = slice(g * GMLP_CHUNK, (g + 1) * GMLP_CHUNK)
            sv = _dot(w, v_ref[rows, cols]) + bias
            o_ref[rows, cols] = (u_ref[rows, cols].astype(F32) * sv).astype(BF16)


def _gmlp(p16, w_spatial, bias_full, *, tm):
    t = p16.shape[0]
    return pl.pallas_call(
        _gmlp_kernel,
        grid=(t // tm,),
        in_specs=[
            pl.BlockSpec((tm, WIDTH), lambda i: (i, SEG_U - N32_SEGS)),
            pl.BlockSpec((tm, WIDTH), lambda i: (i, SEG_V - N32_SEGS)),
            pl.BlockSpec(w_spatial.shape, lambda i: (0, 0, 0)),
            pl.BlockSpec(bias_full.shape, lambda i: (0, 0)),
        ],
        out_specs=pl.BlockSpec((tm, WIDTH), lambda i: (i, 0)),
        out_shape=jax.ShapeDtypeStruct((t, WIDTH), BF16),
        compiler_params=_params("parallel"),
        name="gmlp_spatial",
    )(p16, p16, w_spatial, bias_full)


def _merge_kernel(a_ref, b_ref, ga_ref, gb_ref, x_ref, wa_ref, wb_ref, wo_ref, o_ref, m_ref):
    for n in range(D_MODEL // MXU_COLS):
        cols = slice(n * MXU_COLS, (n + 1) * MXU_COLS)
        ya = _dot(a_ref[...], wa_ref[:, cols])
        yb = _dot(b_ref[...], wb_ref[:, cols])
        m_ref[:, cols] = (ga_ref[:, cols].astype(F32) * ya
                          + gb_ref[:, cols].astype(F32) * yb).astype(BF16)
    for n in range(D_MODEL // MXU_COLS):
        cols = slice(n * MXU_COLS, (n + 1) * MXU_COLS)
        o_ref[:, cols] = x_ref[:, cols] + _dot(m_ref[...], wo_ref[:, cols])


def _merge(oa, ob, p16, x2d, wa, wb, wo, *, tm):
    t = x2d.shape[0]
    row = lambda i: (i, 0)
    fixed = lambda i: (0, 0)
    gate0 = (SEG_GATE - N32_SEGS) * WIDTH // D_MODEL
    return pl.pallas_call(
        _merge_kernel,
        grid=(t // tm,),
        in_specs=[
            pl.BlockSpec((tm, WIDTH), row),
            pl.BlockSpec((tm, WIDTH), row),
            pl.BlockSpec((tm, D_MODEL), lambda i: (i, gate0)),
            pl.BlockSpec((tm, D_MODEL), lambda i: (i, gate0 + 1)),
            pl.BlockSpec((tm, D_MODEL), row),
            pl.BlockSpec(wa.shape, fixed),
            pl.BlockSpec(wb.shape, fixed),
            pl.BlockSpec(wo.shape, fixed),
        ],
        out_specs=pl.BlockSpec((tm, D_MODEL), row),
        out_shape=jax.ShapeDtypeStruct((t, D_MODEL), F32),
        scratch_shapes=[pltpu.VMEM((tm, D_MODEL), BF16)],
        compiler_params=_params("parallel"),
        name="merge_out",
    )(oa, ob, p16, p16, x2d, wa, wb, wo)


def _ffn_kernel(x_ref, gain_ref, w1_ref, w2_ref, o_ref, h_ref):
    j = pl.program_id(1)

    @pl.when(j == 0)
    def _():
        x = x_ref[...]
        h_ref[...] = (_rms(x) * gain_ref[...]).astype(BF16)
        o_ref[...] = x

    a = jnp.maximum(_dot(h_ref[...], w1_ref[...]), 0.0)
    o_ref[...] += _dot((a * a).astype(BF16), w2_ref[...])


def _ffn(x2d, gain, w1, w2, *, tm, tf):
    t = x2d.shape[0]
    row = lambda i, j: (i, 0)
    return pl.pallas_call(
        _ffn_kernel,
        grid=(t // tm, D_FF // tf),
        in_specs=[
            pl.BlockSpec((tm, D_MODEL), row),
            pl.BlockSpec((1, D_MODEL), lambda i, j: (0, 0)),
            pl.BlockSpec((D_MODEL, tf), lambda i, j: (0, j)),
            pl.BlockSpec((tf, D_MODEL), lambda i, j: (j, 0)),
        ],
        out_specs=pl.BlockSpec((tm, D_MODEL), row),
        out_shape=jax.ShapeDtypeStruct((t, D_MODEL), F32),
        scratch_shapes=[pltpu.VMEM((tm, D_MODEL), BF16)],
        compiler_params=_params("parallel", "arbitrary"),
        name="ffn",
    )(x2d, gain, w1, w2)


def _ple_kernel(x_ref, p_ref, gain_ref, wg_ref, wp_ref, gfin_ref, o_ref, h_ref):
    h_ref[...] = (_rms(x_ref[...]) * gain_ref[...]).astype(BF16)
    pb = p_ref[...].astype(BF16)
    sumsq = jnp.zeros((x_ref.shape[0], 1), F32)
    for n in range(D_MODEL // MXU_COLS):
        cols = slice(n * MXU_COLS, (n + 1) * MXU_COLS)
        gate = _sigmoid(_dot(h_ref[...], wg_ref[:, cols]))
        y = x_ref[:, cols] + gate * _dot(pb, wp_ref[:, cols])
        o_ref[:, cols] = y
        sumsq = sumsq + jnp.sum(y * y, axis=-1, keepdims=True)
    scale = lax.rsqrt(sumsq * (1.0 / D_MODEL) + EPS)
    o_ref[...] = o_ref[...] * scale * gfin_ref[...]


def _ple(x2d, p2d, gain, wg, wp, gfin, *, tm):
    t = x2d.shape[0]
    row = lambda i: (i, 0)
    fixed = lambda i: (0, 0)
    return pl.pallas_call(
        _ple_kernel,
        grid=(t // tm,),
        in_specs=[
            pl.BlockSpec((tm, D_MODEL), row),
            pl.BlockSpec((tm, PLE_DIM), row),
            pl.BlockSpec((1, D_MODEL), fixed),
            pl.BlockSpec(wg.shape, fixed),
            pl.BlockSpec(wp.shape, fixed),
            pl.BlockSpec((1, D_MODEL), fixed),
        ],
        out_specs=pl.BlockSpec((tm, D_MODEL), row),
        out_shape=jax.ShapeDtypeStruct((t, D_MODEL), F32),
        scratch_shapes=[pltpu.VMEM((tm, D_MODEL), BF16)],
        compiler_params=_params("parallel"),
        name="ple_final",
    )(x2d, p2d, gain, wg, wp, gfin)


def kernel(x, p, norm_mix, w_in, lb_logits, hgrn_norm, w_a_out, gmlp_ln_g, gmlp_ln_b,
           w_spatial, b_spatial, w_b_out, w_o, norm_ffn, w_ff1, w_ff2, norm_ple,
           w_ple_gate, w_ple_proj, norm_final):
    batch, seq, _ = x.shape
    assert w_in.shape[0] == 1 and seq % GMLP_CHUNK == 0
    t = batch * seq
    x2d = x.reshape(t, D_MODEL)
    row = lambda a: a.reshape(1, -1)

    p32, p16 = _inproj(x2d, row(norm_mix[0]), w_in[0], lb_logits,
                       row(gmlp_ln_g[0]), row(gmlp_ln_b[0]), tm=1024)
    o_a, (wa, wb, wo, w1, w2, wg) = _hgrn(
        p32, p16, row(hgrn_norm[0]),
        [w_a_out[0], w_b_out[0], w_o[0], w_ff1[0], w_ff2[0], w_ple_gate[0]], batch, seq)
    bias_full = jnp.repeat(b_spatial[0].T, HEAD_DIM, axis=1)
    o_b = _gmlp(p16, w_spatial[0], bias_full, tm=512)
    x2d = _merge(o_a, o_b, p16, x2d, wa, wb, wo, tm=512)
    x2d = _ffn(x2d, row(norm_ffn[0]), w1, w2, tm=512, tf=1024)
    x2d = _ple(x2d, p[0].reshape(t, PLE_DIM), row(norm_ple[0]), wg,
               w_ple_proj[0].astype(BF16), row(norm_final), tm=512)
    return x2d.reshape(batch, seq, D_MODEL)
```

```python
import functools

import jax
import jax.numpy as jnp
import numpy as np
from jax import lax
from jax.experimental import pallas as pl
from jax.experimental.pallas import tpu as pltpu

F32 = jnp.float32
BF16 = jnp.bfloat16

D_MODEL = 2048
WIDTH = D_MODEL // 2
HEAD_DIM = 128
HEADS = WIDTH // HEAD_DIM
CHUNK = 64
GMLP_CHUNK = 128
D_FF = 4 * D_MODEL
PLE_DIM = 256
EPS = 1e-6
LOG2_E = np.float32(np.log2(np.e))
N_IN = 6 * WIDTH + 2 * D_MODEL

SCAN_LEVELS = 6
VMEM_LIMIT_BYTES = 56 * 1024 * 1024
MXU_COLS = 256
SCAN_GROUP = 4

SEG_Q, SEG_F, SEG_INP, SEG_G, SEG_U, SEG_V, SEG_GATE = 0, 1, 2, 3, 4, 5, 6
N32_SEGS = 2


def _rms(x):
    return x * lax.rsqrt(jnp.mean(x * x, axis=-1, keepdims=True) + EPS)


def _dot(a, b):
    return jnp.dot(a, b, preferred_element_type=F32)


def _dot_nt(a, b):
    return lax.dot_general(a, b, (((1,), (1,)), ((), ())), preferred_element_type=F32)


def _dot_tn(a, b):
    return lax.dot_general(a, b, (((0,), (0,)), ((), ())), preferred_element_type=F32)


def _gelu(x):
    return x * (lax.erf(x * np.float32(1.0 / np.sqrt(2.0))) + 1.0) * 0.5


def _sigmoid(x):
    return 0.5 * jnp.tanh(0.5 * x) + 0.5


def _params(*semantics):
    return pltpu.CompilerParams(
        dimension_semantics=semantics, vmem_limit_bytes=VMEM_LIMIT_BYTES)


def _inproj_kernel(x_ref, gain_ref, w_ref, lbl_ref, lng_ref, lnb_ref, o32_ref, o16_ref, h_ref):
    j = pl.program_id(1)

    @pl.when(j == 0)
    def _():
        h_ref[...] = (_rms(x_ref[...]) * gain_ref[...]).astype(BF16)

    def emit(o_ref, act):
        for n in range(WIDTH // MXU_COLS):
            cols = slice(n * MXU_COLS, (n + 1) * MXU_COLS)
            o_ref[:, cols] = act(_dot(h_ref[...], w_ref[:, cols].astype(BF16)), cols).astype(o_ref.dtype)

    silu = lambda a, cols: a * _sigmoid(a)

    @pl.when(j == SEG_Q)
    def _():
        emit(o32_ref, silu)

    @pl.when(j == SEG_F)
    def _():
        logits = lbl_ref[...]
        e = jnp.exp(logits - jnp.max(logits, axis=0, keepdims=True))
        lb = (e / jnp.sum(e, axis=0, keepdims=True))[0:1, :]
        emit(o32_ref, lambda a, cols: LOG2_E * jnp.log(
            lb[:, cols] + (1.0 - lb[:, cols]) * jax.nn.sigmoid(a)))

    @pl.when(j == SEG_INP)
    def _():
        emit(o16_ref, lambda a, cols: a)

    @pl.when(j == SEG_G)
    def _():
        emit(o16_ref, silu)

    @pl.when(j == SEG_U)
    def _():
        emit(o16_ref, lambda a, cols: _gelu(a))

    @pl.when(j == SEG_V)
    def _():
        a = _gelu(_dot(h_ref[...], w_ref[...].astype(BF16)))
        mu = jnp.mean(a, axis=-1, keepdims=True)
        c = a - mu
        var = jnp.mean(c * c, axis=-1, keepdims=True)
        o16_ref[...] = (c * lax.rsqrt(var + EPS) * lng_ref[...] + lnb_ref[...]).astype(BF16)

    @pl.when(j >= SEG_GATE)
    def _():
        emit(o16_ref, lambda a, cols: _sigmoid(a))


def _inproj(x2d, gain, w, lb_logits, ln_g, ln_b, *, tm):
    t = x2d.shape[0]
    row = lambda i, j: (i, 0)
    fixed = lambda i, j: (0, 0)
    n32 = N32_SEGS * WIDTH
    return pl.pallas_call(
        _inproj_kernel,
        grid=(t // tm, N_IN // WIDTH),
        in_specs=[
            pl.BlockSpec((tm, D_MODEL), row),
            pl.BlockSpec((1, D_MODEL), fixed),
            pl.BlockSpec((D_MODEL, WIDTH), lambda i, j: (0, j)),
            pl.BlockSpec(lb_logits.shape, fixed),
            pl.BlockSpec((1, WIDTH), fixed),
            pl.BlockSpec((1, WIDTH), fixed),
        ],
        out_specs=[
            pl.BlockSpec((tm, WIDTH), lambda i, j: (i, jnp.minimum(j, N32_SEGS - 1))),
            pl.BlockSpec((tm, WIDTH), lambda i, j: (i, jnp.maximum(j - N32_SEGS, 0))),
        ],
        out_shape=[jax.ShapeDtypeStruct((t, n32), F32),
                   jax.ShapeDtypeStruct((t, N_IN - n32), BF16)],
        scratch_shapes=[pltpu.VMEM((tm, D_MODEL), BF16)],
        compiler_params=_params("parallel", "arbitrary"),
        name="inproj",
    )(x2d, gain, w, lb_logits, ln_g, ln_b)


def _pair_masks():
    t = np.arange(CHUNK)[:, None]
    s = np.arange(CHUNK)[None, :]
    masks = []
    for lvl in range(SCAN_LEVELS):
        same = (t >> (lvl + 1)) == (s >> (lvl + 1))
        masks.append(same & (((t >> lvl) & 1) == 1) & (((s >> lvl) & 1) == 0))
    masks.append(t == s)
    return np.stack(masks).astype(np.float32)


def _level_exponents(p, lvl):
    d = p.shape[-1]
    half = 1 << lvl
    if half >= 8:
        p4 = p.reshape(CHUNK // (2 * half), 2, half, d)
        lo, up = p4[:, 0], p4[:, 1]
        ref = jnp.broadcast_to(lo[:, half - 1:half, :], lo.shape)
        e = jnp.stack([ref - lo, up], axis=1).reshape(CHUNK, d)
        return e, jnp.stack([lo, up + ref], axis=1).reshape(CHUNK, d)
    sub = lax.broadcasted_iota(jnp.int32, (CHUNK, d), 0)
    upper = ((sub >> lvl) & 1) == 1
    prev1 = pltpu.roll(p, 1, axis=0)
    if lvl == 0:
        return jnp.where(upper, p, 0.0), p + jnp.where(upper, prev1, 0.0)
    if lvl == 1:
        pos = sub & 3
        nxt1 = pltpu.roll(p, CHUNK - 1, axis=0)
        prev2 = pltpu.roll(p, 2, axis=0)
        ref = jnp.where(pos == 0, nxt1, jnp.where(pos == 1, p, jnp.where(pos == 2, prev1, prev2)))
    else:
        p3 = p.reshape(CHUNK // 8, 8, d)
        ref = jnp.broadcast_to(p3[:, 3:4, :], p3.shape).reshape(CHUNK, d)
    return jnp.where(upper, p, ref - p), p + jnp.where(upper, ref, 0.0)


def _hgrn_kernel(n_side, q_ref, lf_ref, v_ref, g_ref, mask_ref, gn_ref, *refs):
    for src_ref, dst_ref in zip(refs[:n_side], refs[n_side + 1:2 * n_side + 1]):
        dst_ref[...] = src_ref[...].astype(BF16)
    o_ref = refs[n_side]
    qs_ref, ks_ref, dec_ref, at_ref, st_ref = refs[2 * n_side + 1:]
    n_groups = q_ref.shape[0] // (CHUNK * SCAN_GROUP)
    gn = gn_ref[...]

    def chunk_rows(c):
        start = c * CHUNK
        return pl.ds(start if isinstance(c, int) else pl.multiple_of(start, CHUNK), CHUNK)

    def scale(c):
        rows = chunk_rows(c)
        lf = lf_ref[rows, :]
        qb = q_ref[rows, :].astype(BF16)
        kb = (1.0 - jnp.exp2(lf)).astype(BF16)
        qs_ref[0, rows, :] = qb
        ks_ref[0, rows, :] = kb
        p = lf
        for lvl in range(SCAN_LEVELS):
            e, p = _level_exponents(p, lvl)
            wb = jnp.exp2(e).astype(BF16)
            qs_ref[1 + lvl, rows, :] = qb * wb
            if lvl > 0:
                ks_ref[lvl, rows, :] = kb * wb
        last = p[CHUNK - 1:CHUNK, :]
        qs_ref[1 + SCAN_LEVELS, rows, :] = qb * jnp.exp2(p).astype(BF16)
        ks_ref[SCAN_LEVELS, rows, :] = kb * jnp.exp2(last - p).astype(BF16)
        dec_ref[c] = jnp.broadcast_to(jnp.exp2(last), (8, HEAD_DIM))

    def scores(c, st):
        rows = chunk_rows(c)
        s_diag = _dot_nt(qs_ref[0, rows, :], ks_ref[0, rows, :])
        s_lvl = [_dot_nt(qs_ref[1 + lvl, rows, :], ks_ref[lvl, rows, :])
                 for lvl in range(SCAN_LEVELS)]
        tiles = []
        for r in range(CHUNK // 8):
            sl = slice(8 * r, 8 * r + 8)
            acc = mask_ref[SCAN_LEVELS, sl, :] * s_diag[sl]
            for lvl in range(SCAN_LEVELS):
                if lvl < 3 or (r >> (lvl - 3)) & 1:
                    acc = acc + mask_ref[lvl, sl, :] * s_lvl[lvl][sl]
            tiles.append(acc)
        at_ref[rows, :] = jnp.concatenate(tiles, axis=0).astype(BF16)
        st_ref[c] = st.astype(BF16)
        return st * dec_ref[c][0:1, :] + _dot_tn(v_ref[rows, :], ks_ref[SCAN_LEVELS, rows, :])

    def output(c):
        rows = chunk_rows(c)
        o = _dot_nt(qs_ref[1 + SCAN_LEVELS, rows, :], st_ref[c])
        o = o + _dot(at_ref[rows, :], v_ref[rows, :])
        o_ref[rows, :] = (_rms(o) * gn * g_ref[rows, :].astype(F32)).astype(BF16)

    def stage(group, st, do_output, do_scores, do_scale):
        for i in range(SCAN_GROUP):
            if do_output:
                output((group - 2) * SCAN_GROUP + i)
        for i in range(SCAN_GROUP):
            if do_scores:
                st = scores((group - 1) * SCAN_GROUP + i, st)
        for i in range(SCAN_GROUP):
            if do_scale:
                scale(group * SCAN_GROUP + i)
        return st

    st = jnp.zeros((HEAD_DIM, HEAD_DIM), F32)
    st = stage(0, st, False, False, True)
    st = stage(1, st, False, True, True)
    st = lax.fori_loop(2, n_groups, lambda g, s: stage(g, s, True, True, True), st)
    st = stage(n_groups, st, True, True, False)
    stage(n_groups + 1, st, True, False, False)


def _hgrn(p32, p16, gn, side_weights, batch, seq):
    masks = _pair_masks()
    steps = batch * HEADS
    blk = lambda seg: pl.BlockSpec((seq, HEAD_DIM), lambda b, h: (b, seg * HEADS + h))
    side_specs = [pl.BlockSpec((w.shape[0] // steps, w.shape[1]), lambda b, h: (b * HEADS + h, 0))
                  for w in side_weights]
    outs = pl.pallas_call(
        functools.partial(_hgrn_kernel, len(side_weights)),
        grid=(batch, HEADS),
        in_specs=[
            blk(SEG_Q), blk(SEG_F), blk(SEG_INP - N32_SEGS), blk(SEG_G - N32_SEGS),
            pl.BlockSpec(masks.shape, lambda b, h: (0, 0, 0)),
            pl.BlockSpec((1, HEAD_DIM), lambda b, h: (0, 0)),
        ] + side_specs,
        out_specs=[pl.BlockSpec((seq, HEAD_DIM), lambda b, h: (b, h))] + side_specs,
        out_shape=[jax.ShapeDtypeStruct((batch * seq, WIDTH), BF16)]
        + [jax.ShapeDtypeStruct(w.shape, BF16) for w in side_weights],
        scratch_shapes=[
            pltpu.VMEM((SCAN_LEVELS + 2, seq, HEAD_DIM), BF16),
            pltpu.VMEM((SCAN_LEVELS + 1, seq, HEAD_DIM), BF16),
            pltpu.VMEM((seq // CHUNK, 8, HEAD_DIM), F32),
            pltpu.VMEM((seq, CHUNK), BF16),
            pltpu.VMEM((seq // CHUNK, HEAD_DIM, HEAD_DIM), BF16),
        ],
        compiler_params=_params("parallel", "arbitrary"),
        name="hgrn_scan",
    )(p32, p32, p16, p16, jnp.asarray(masks), gn, *side_weights)
    return outs[0], outs[1:]


def _gmlp_kernel(u_ref, v_ref, w_ref, bias_ref, o_ref):
    t = lax.broadcasted_iota(jnp.int32, (GMLP_CHUNK, GMLP_CHUNK), 0)
    s = lax.broadcasted_iota(jnp.int32, (GMLP_CHUNK, GMLP_CHUNK), 1)
    keep = (s // CHUNK) <= (t // CHUNK)
    for h in range(HEADS):
        cols = slice(h * HEAD_DIM, (h + 1) * HEAD_DIM)
        w = jnp.where(keep, w_ref[h], 0.0).astype(BF16)
        bias = bias_ref[:, cols]
        for g in range(u_ref.shape[0] // GMLP_CHUNK):
            rows = slice(g * GMLP_CHUNK, (g + 1) * GMLP_CHUNK)
            sv = _dot(w, v_ref[rows, cols]) + bias
            o_ref[rows, cols] = (u_ref[rows, cols].astype(F32) * sv).astype(BF16)


def _gmlp(p16, w_spatial, bias_full, *, tm):
    t = p16.shape[0]
    return pl.pallas_call(
        _gmlp_kernel,
        grid=(t // tm,),
        in_specs=[
            pl.BlockSpec((tm, WIDTH), lambda i: (i, SEG_U - N32_SEGS)),
            pl.BlockSpec((tm, WIDTH), lambda i: (i, SEG_V - N32_SEGS)),
            pl.BlockSpec(w_spatial.shape, lambda i: (0, 0, 0)),
            pl.BlockSpec(bias_full.shape, lambda i: (0, 0)),
        ],
        out_specs=pl.BlockSpec((tm, WIDTH), lambda i: (i, 0)),
        out_shape=jax.ShapeDtypeStruct((t, WIDTH), BF16),
        compiler_params=_params("parallel"),
        name="gmlp_spatial",
    )(p16, p16, w_spatial, bias_full)


def _merge_kernel(a_ref, b_ref, ga_ref, gb_ref, x_ref, wa_ref, wb_ref, wo_ref, o_ref, m_ref):
    for n in range(D_MODEL // MXU_COLS):
        cols = slice(n * MXU_COLS, (n + 1) * MXU_COLS)
        ya = _dot(a_ref[...], wa_ref[:, cols])
        yb = _dot(b_ref[...], wb_ref[:, cols])
        m_ref[:, cols] = (ga_ref[:, cols].astype(F32) * ya
                          + gb_ref[:, cols].astype(F32) * yb).astype(BF16)
    for n in range(D_MODEL // MXU_COLS):
        cols = slice(n * MXU_COLS, (n + 1) * MXU_COLS)
        o_ref[:, cols] = x_ref[:, cols] + _dot(m_ref[...], wo_ref[:, cols])


def _merge(oa, ob, p16, x2d, wa, wb, wo, *, tm):
    t = x2d.shape[0]
    row = lambda i: (i, 0)
    fixed = lambda i: (0, 0)
    gate0 = (SEG_GATE - N32_SEGS) * WIDTH // D_MODEL
    return pl.pallas_call(
        _merge_kernel,
        grid=(t // tm,),
        in_specs=[
            pl.BlockSpec((tm, WIDTH), row),
            pl.BlockSpec((tm, WIDTH), row),
            pl.BlockSpec((tm, D_MODEL), lambda i: (i, gate0)),
            pl.BlockSpec((tm, D_MODEL), lambda i: (i, gate0 + 1)),
            pl.BlockSpec((tm, D_MODEL), row),
            pl.BlockSpec(wa.shape, fixed),
            pl.BlockSpec(wb.shape, fixed),
            pl.BlockSpec(wo.shape, fixed),
        ],
        out_specs=pl.BlockSpec((tm, D_MODEL), row),
        out_shape=jax.ShapeDtypeStruct((t, D_MODEL), F32),
        scratch_shapes=[pltpu.VMEM((tm, D_MODEL), BF16)],
        compiler_params=_params("parallel"),
        name="merge_out",
    )(oa, ob, p16, p16, x2d, wa, wb, wo)


def _ffn_kernel(x_ref, gain_ref, w1_ref, w2_ref, o_ref, h_ref):
    j = pl.program_id(1)

    @pl.when(j == 0)
    def _():
        x = x_ref[...]
        h_ref[...] = (_rms(x) * gain_ref[...]).astype(BF16)
        o_ref[...] = x

    a = jnp.maximum(_dot(h_ref[...], w1_ref[...]), 0.0)
    o_ref[...] += _dot((a * a).astype(BF16), w2_ref[...])


def _ffn(x2d, gain, w1, w2, *, tm, tf):
    t = x2d.shape[0]
    row = lambda i, j: (i, 0)
    return pl.pallas_call(
        _ffn_kernel,
        grid=(t // tm, D_FF // tf),
        in_specs=[
            pl.BlockSpec((tm, D_MODEL), row),
            pl.BlockSpec((1, D_MODEL), lambda i, j: (0, 0)),
            pl.BlockSpec((D_MODEL, tf), lambda i, j: (0, j)),
            pl.BlockSpec((tf, D_MODEL), lambda i, j: (j, 0)),
        ],
        out_specs=pl.BlockSpec((tm, D_MODEL), row),
        out_shape=jax.ShapeDtypeStruct((t, D_MODEL), F32),
        scratch_shapes=[pltpu.VMEM((tm, D_MODEL), BF16)],
        compiler_params=_params("parallel", "arbitrary"),
        name="ffn",
    )(x2d, gain, w1, w2)


def _ple_kernel(x_ref, p_ref, gain_ref, wg_ref, wp_ref, gfin_ref, o_ref, h_ref):
    h_ref[...] = (_rms(x_ref[...]) * gain_ref[...]).astype(BF16)
    pb = p_ref[...].astype(BF16)
    sumsq = jnp.zeros((x_ref.shape[0], 1), F32)
    for n in range(D_MODEL // MXU_COLS):
        cols = slice(n * MXU_COLS, (n + 1) * MXU_COLS)
        gate = _sigmoid(_dot(h_ref[...], wg_ref[:, cols]))
        y = x_ref[:, cols] + gate * _dot(pb, wp_ref[:, cols])
        o_ref[:, cols] = y
        sumsq = sumsq + jnp.sum(y * y, axis=-1, keepdims=True)
    scale = lax.rsqrt(sumsq * (1.0 / D_MODEL) + EPS)
    o_ref[...] = o_ref[...] * scale * gfin_ref[...]


def _ple(x2d, p2d, gain, wg, wp, gfin, *, tm):
    t = x2d.shape[0]
    row = lambda i: (i, 0)
    fixed = lambda i: (0, 0)
    return pl.pallas_call(
        _ple_kernel,
        grid=(t // tm,),
        in_specs=[
            pl.BlockSpec((tm, D_MODEL), row),
            pl.BlockSpec((tm, PLE_DIM), row),
            pl.BlockSpec((1, D_MODEL), fixed),
            pl.BlockSpec(wg.shape, fixed),
            pl.BlockSpec(wp.shape, fixed),
            pl.BlockSpec((1, D_MODEL), fixed),
        ],
        out_specs=pl.BlockSpec((tm, D_MODEL), row),
        out_shape=jax.ShapeDtypeStruct((t, D_MODEL), F32),
        scratch_shapes=[pltpu.VMEM((tm, D_MODEL), BF16)],
        compiler_params=_params("parallel"),
        name="ple_final",
    )(x2d, p2d, gain, wg, wp, gfin)


def kernel(x, p, norm_mix, w_in, lb_logits, hgrn_norm, w_a_out, gmlp_ln_g, gmlp_ln_b,
           w_spatial, b_spatial, w_b_out, w_o, norm_ffn, w_ff1, w_ff2, norm_ple,
           w_ple_gate, w_ple_proj, norm_final):
    batch, seq, _ = x.shape
    assert w_in.shape[0] == 1 and seq % GMLP_CHUNK == 0
    t = batch * seq
    x2d = x.reshape(t, D_MODEL)
    row = lambda a: a.reshape(1, -1)

    p32, p16 = _inproj(x2d, row(norm_mix[0]), w_in[0], lb_logits,
                       row(gmlp_ln_g[0]), row(gmlp_ln_b[0]), tm=1024)
    o_a, (wa, wb, wo, w1, w2, wg) = _hgrn(
        p32, p16, row(hgrn_norm[0]),
        [w_a_out[0], w_b_out[0], w_o[0], w_ff1[0], w_ff2[0], w_ple_gate[0]], batch, seq)
    bias_full = jnp.repeat(b_spatial[0].T, HEAD_DIM, axis=1)
    o_b = _gmlp(p16, w_spatial[0], bias_full, tm=512)
    x2d = _merge(o_a, o_b, p16, x2d, wa, wb, wo, tm=512)
    x2d = _ffn(x2d, row(norm_ffn[0]), w1, w2, tm=512, tf=1024)
    x2d = _ple(x2d, p[0].reshape(t, PLE_DIM), row(norm_ple[0]), wg,
               w_ple_proj[0].astype(BF16), row(norm_final), tm=512)
    return x2d.reshape(batch, seq, D_MODEL)
```

```python
import functools

import jax
import jax.numpy as jnp
import numpy as np
from jax import lax
from jax.experimental import pallas as pl
from jax.experimental.pallas import tpu as pltpu

F32 = jnp.float32
BF16 = jnp.bfloat16

D_MODEL = 2048
WIDTH = D_MODEL // 2
HEAD_DIM = 128
HEADS = WIDTH // HEAD_DIM
CHUNK = 64
GMLP_CHUNK = 128
D_FF = 4 * D_MODEL
PLE_DIM = 256
EPS = 1e-6
LOG2_E = np.float32(np.log2(np.e))
N_IN = 6 * WIDTH + 2 * D_MODEL

SCAN_LEVELS = 6
VMEM_LIMIT_BYTES = 56 * 1024 * 1024
MXU_COLS = 256
SCAN_GROUP = 4

SEG_Q, SEG_F, SEG_INP, SEG_G, SEG_U, SEG_V, SEG_GATE = 0, 1, 2, 3, 4, 5, 6
N32_SEGS = 2


def _rms(x):
    return x * lax.rsqrt(jnp.mean(x * x, axis=-1, keepdims=True) + EPS)


def _dot(a, b):
    return jnp.dot(a, b, preferred_element_type=F32)


def _dot_nt(a, b):
    return lax.dot_general(a, b, (((1,), (1,)), ((), ())), preferred_element_type=F32)


def _dot_tn(a, b):
    return lax.dot_general(a, b, (((0,), (0,)), ((), ())), preferred_element_type=F32)


def _gelu(x):
    return x * (lax.erf(x * np.float32(1.0 / np.sqrt(2.0))) + 1.0) * 0.5


def _sigmoid(x):
    return 0.5 * jnp.tanh(0.5 * x) + 0.5


def _params(*semantics):
    return pltpu.CompilerParams(
        dimension_semantics=semantics, vmem_limit_bytes=VMEM_LIMIT_BYTES)


def _inproj_kernel(x_ref, gain_ref, w_ref, lbl_ref, lng_ref, lnb_ref, o32_ref, o16_ref, h_ref):
    j = pl.program_id(1)

    @pl.when(j == 0)
    def _():
        h_ref[...] = (_rms(x_ref[...]) * gain_ref[...]).astype(BF16)

    def emit(o_ref, act):
        for n in range(WIDTH // MXU_COLS):
            cols = slice(n * MXU_COLS, (n + 1) * MXU_COLS)
            o_ref[:, cols] = act(_dot(h_ref[...], w_ref[:, cols].astype(BF16)), cols).astype(o_ref.dtype)

    silu = lambda a, cols: a * _sigmoid(a)

    @pl.when(j == SEG_Q)
    def _():
        emit(o32_ref, silu)

    @pl.when(j == SEG_F)
    def _():
        logits = lbl_ref[...]
        e = jnp.exp(logits - jnp.max(logits, axis=0, keepdims=True))
        lb = (e / jnp.sum(e, axis=0, keepdims=True))[0:1, :]
        emit(o32_ref, lambda a, cols: LOG2_E * jnp.log(
            lb[:, cols] + (1.0 - lb[:, cols]) * jax.nn.sigmoid(a)))

    @pl.when(j == SEG_INP)
    def _():
        emit(o16_ref, lambda a, cols: a)

    @pl.when(j == SEG_G)
    def _():
        emit(o16_ref, silu)

    @pl.when(j == SEG_U)
    def _():
        emit(o16_ref, lambda a, cols: _gelu(a))

    @pl.when(j == SEG_V)
    def _():
        a = _gelu(_dot(h_ref[...], w_ref[...].astype(BF16)))
        mu = jnp.mean(a, axis=-1, keepdims=True)
        c = a - mu
        var = jnp.mean(c * c, axis=-1, keepdims=True)
        o16_ref[...] = (c * lax.rsqrt(var + EPS) * lng_ref[...] + lnb_ref[...]).astype(BF16)

    @pl.when(j >= SEG_GATE)
    def _():
        emit(o16_ref, lambda a, cols: _sigmoid(a))


def _inproj(x2d, gain, w, lb_logits, ln_g, ln_b, *, tm):
    t = x2d.shape[0]
    row = lambda i, j: (i, 0)
    fixed = lambda i, j: (0, 0)
    n32 = N32_SEGS * WIDTH
    return pl.pallas_call(
        _inproj_kernel,
        grid=(t // tm, N_IN // WIDTH),
        in_specs=[
            pl.BlockSpec((tm, D_MODEL), row),
            pl.BlockSpec((1, D_MODEL), fixed),
            pl.BlockSpec((D_MODEL, WIDTH), lambda i, j: (0, j)),
            pl.BlockSpec(lb_logits.shape, fixed),
            pl.BlockSpec((1, WIDTH), fixed),
            pl.BlockSpec((1, WIDTH), fixed),
        ],
        out_specs=[
            pl.BlockSpec((tm, WIDTH), lambda i, j: (i, jnp.minimum(j, N32_SEGS - 1))),
            pl.BlockSpec((tm, WIDTH), lambda i, j: (i, jnp.maximum(j - N32_SEGS, 0))),
        ],
        out_shape=[jax.ShapeDtypeStruct((t, n32), F32),
                   jax.ShapeDtypeStruct((t, N_IN - n32), BF16)],
        scratch_shapes=[pltpu.VMEM((tm, D_MODEL), BF16)],
        compiler_params=_params("parallel", "arbitrary"),
        name="inproj",
    )(x2d, gain, w, lb_logits, ln_g, ln_b)


def _pair_masks():
    t = np.arange(CHUNK)[:, None]
    s = np.arange(CHUNK)[None, :]
    masks = []
    for lvl in range(SCAN_LEVELS):
        same = (t >> (lvl + 1)) == (s >> (lvl + 1))
        masks.append(same & (((t >> lvl) & 1) == 1) & (((s >> lvl) & 1) == 0))
    masks.append(t == s)
    return np.stack(masks).astype(np.float32)


def _level_exponents(p, lvl):
    d = p.shape[-1]
    half = 1 << lvl
    if half >= 8:
        p4 = p.reshape(CHUNK // (2 * half), 2, half, d)
        lo, up = p4[:, 0], p4[:, 1]
        ref = jnp.broadcast_to(lo[:, half - 1:half, :], lo.shape)
        e = jnp.stack([ref - lo, up], axis=1).reshape(CHUNK, d)
        return e, jnp.stack([lo, up + ref], axis=1).reshape(CHUNK, d)
    sub = lax.broadcasted_iota(jnp.int32, (CHUNK, d), 0)
    upper = ((sub >> lvl) & 1) == 1
    prev1 = pltpu.roll(p, 1, axis=0)
    if lvl == 0:
        return jnp.where(upper, p, 0.0), p + jnp.where(upper, prev1, 0.0)
    if lvl == 1:
        pos = sub & 3
        nxt1 = pltpu.roll(p, CHUNK - 1, axis=0)
        prev2 = pltpu.roll(p, 2, axis=0)
        ref = jnp.where(pos == 0, nxt1, jnp.where(pos == 1, p, jnp.where(pos == 2, prev1, prev2)))
    else:
        p3 = p.reshape(CHUNK // 8, 8, d)
        ref = jnp.broadcast_to(p3[:, 3:4, :], p3.shape).reshape(CHUNK, d)
    return jnp.where(upper, p, ref - p), p + jnp.where(upper, ref, 0.0)


def _mixers_kernel(n_side, q_ref, lf_ref, v_ref, g_ref, mask_ref, gn_ref,
                   u_ref, gv_ref, ws_ref, bias_ref, *refs):
    for src_ref, dst_ref in zip(refs[:n_side], refs[n_side + 2:2 * n_side + 2]):
        dst_ref[...] = src_ref[...].astype(BF16)
    o_ref, ob_ref = refs[n_side], refs[n_side + 1]
    qs_ref, ks_ref, dec_ref, at_ref, st_ref = refs[2 * n_side + 2:]
    _gmlp_rows(u_ref, gv_ref, ws_ref, bias_ref, ob_ref)
    n_groups = q_ref.shape[0] // (CHUNK * SCAN_GROUP)
    gn = gn_ref[...]

    def chunk_rows(c):
        start = c * CHUNK
        return pl.ds(start if isinstance(c, int) else pl.multiple_of(start, CHUNK), CHUNK)

    def scale(c):
        rows = chunk_rows(c)
        lf = lf_ref[rows, :]
        qb = q_ref[rows, :].astype(BF16)
        kb = (1.0 - jnp.exp2(lf)).astype(BF16)
        qs_ref[0, rows, :] = qb
        ks_ref[0, rows, :] = kb
        p = lf
        for lvl in range(SCAN_LEVELS):
            e, p = _level_exponents(p, lvl)
            wb = jnp.exp2(e).astype(BF16)
            qs_ref[1 + lvl, rows, :] = qb * wb
            if lvl > 0:
                ks_ref[lvl, rows, :] = kb * wb
        last = p[CHUNK - 1:CHUNK, :]
        qs_ref[1 + SCAN_LEVELS, rows, :] = qb * jnp.exp2(p).astype(BF16)
        ks_ref[SCAN_LEVELS, rows, :] = kb * jnp.exp2(last - p).astype(BF16)
        dec_ref[c] = jnp.broadcast_to(jnp.exp2(last), (8, HEAD_DIM))

    def scores(c, st):
        rows = chunk_rows(c)
        s_diag = _dot_nt(qs_ref[0, rows, :], ks_ref[0, rows, :])
        s_lvl = [_dot_nt(qs_ref[1 + lvl, rows, :], ks_ref[lvl, rows, :])
                 for lvl in range(SCAN_LEVELS)]
        tiles = []
        for r in range(CHUNK // 8):
            sl = slice(8 * r, 8 * r + 8)
            acc = mask_ref[SCAN_LEVELS, sl, :] * s_diag[sl]
            for lvl in range(SCAN_LEVELS):
                if lvl < 3 or (r >> (lvl - 3)) & 1:
                    acc = acc + mask_ref[lvl, sl, :] * s_lvl[lvl][sl]
            tiles.append(acc)
        at_ref[rows, :] = jnp.concatenate(tiles, axis=0).astype(BF16)
        st_ref[c] = st.astype(BF16)
        return st * dec_ref[c][0:1, :] + _dot_tn(v_ref[rows, :], ks_ref[SCAN_LEVELS, rows, :])

    def output(c):
        rows = chunk_rows(c)
        o = _dot_nt(qs_ref[1 + SCAN_LEVELS, rows, :], st_ref[c])
        o = o + _dot(at_ref[rows, :], v_ref[rows, :])
        o_ref[rows, :] = (_rms(o) * gn * g_ref[rows, :].astype(F32)).astype(BF16)

    def stage(group, st, do_output, do_scores, do_scale):
        for i in range(SCAN_GROUP):
            if do_output:
                output((group - 2) * SCAN_GROUP + i)
        for i in range(SCAN_GROUP):
            if do_scores:
                st = scores((group - 1) * SCAN_GROUP + i, st)
        for i in range(SCAN_GROUP):
            if do_scale:
                scale(group * SCAN_GROUP + i)
        return st

    st = jnp.zeros((HEAD_DIM, HEAD_DIM), F32)
    st = stage(0, st, False, False, True)
    st = stage(1, st, False, True, True)
    st = lax.fori_loop(2, n_groups, lambda g, s: stage(g, s, True, True, True), st)
    st = stage(n_groups, st, True, True, False)
    stage(n_groups + 1, st, True, False, False)


def _mixers(p32, p16, gn, w_spatial, bias_full, side_weights, batch, seq):
    masks = _pair_masks()
    steps = batch * HEADS
    t = batch * seq
    blk = lambda seg: pl.BlockSpec((seq, HEAD_DIM), lambda b, h: (b, seg * HEADS + h))
    step_rows = lambda b, h: (b * HEADS + h, 0)
    gmlp_blk = lambda seg: pl.BlockSpec((t // steps, WIDTH), lambda b, h: (b * HEADS + h, seg))
    side_specs = [pl.BlockSpec((w.shape[0] // steps, w.shape[1]), step_rows)
                  for w in side_weights]
    outs = pl.pallas_call(
        functools.partial(_mixers_kernel, len(side_weights)),
        grid=(batch, HEADS),
        in_specs=[
            blk(SEG_Q), blk(SEG_F), blk(SEG_INP - N32_SEGS), blk(SEG_G - N32_SEGS),
            pl.BlockSpec(masks.shape, lambda b, h: (0, 0, 0)),
            pl.BlockSpec((1, HEAD_DIM), lambda b, h: (0, 0)),
            gmlp_blk(SEG_U - N32_SEGS), gmlp_blk(SEG_V - N32_SEGS),
            pl.BlockSpec(w_spatial.shape, lambda b, h: (0, 0, 0)),
            pl.BlockSpec(bias_full.shape, lambda b, h: (0, 0)),
        ] + side_specs,
        out_specs=[pl.BlockSpec((seq, HEAD_DIM), lambda b, h: (b, h)),
                   pl.BlockSpec((t // steps, WIDTH), step_rows)] + side_specs,
        out_shape=[jax.ShapeDtypeStruct((t, WIDTH), BF16), jax.ShapeDtypeStruct((t, WIDTH), BF16)]
        + [jax.ShapeDtypeStruct(w.shape, BF16) for w in side_weights],
        scratch_shapes=[
            pltpu.VMEM((SCAN_LEVELS + 2, seq, HEAD_DIM), BF16),
            pltpu.VMEM((SCAN_LEVELS + 1, seq, HEAD_DIM), BF16),
            pltpu.VMEM((seq // CHUNK, 8, HEAD_DIM), F32),
            pltpu.VMEM((seq, CHUNK), BF16),
            pltpu.VMEM((seq // CHUNK, HEAD_DIM, HEAD_DIM), BF16),
        ],
        compiler_params=_params("parallel", "arbitrary"),
        name="mixers",
    )(p32, p32, p16, p16, jnp.asarray(masks), gn, p16, p16, w_spatial, bias_full, *side_weights)
    return outs[0], outs[1], outs[2:]


def _gmlp_rows(u_ref, v_ref, w_ref, bias_ref, o_ref):
    t = lax.broadcasted_iota(jnp.int32, (GMLP_CHUNK, GMLP_CHUNK), 0)
    s = lax.broadcasted_iota(jnp.int32, (GMLP_CHUNK, GMLP_CHUNK), 1)
    keep = (s // CHUNK) <= (t // CHUNK)
    for h in range(HEADS):
        cols = slice(h * HEAD_DIM, (h + 1) * HEAD_DIM)
        w = jnp.where(keep, w_ref[h], 0.0).astype(BF16)
        bias = bias_ref[:, cols]
        for g in range(u_ref.shape[0] // GMLP_CHUNK):
            rows = slice(g * GMLP_CHUNK, (g + 1) * GMLP_CHUNK)
            sv = _dot(w, v_ref[rows, cols]) + bias
            o_ref[rows, cols] = (u_ref[rows, cols].astype(F32) * sv).astype(BF16)


def _merge_kernel(a_ref, b_ref, ga_ref, gb_ref, x_ref, wa_ref, wb_ref, wo_ref, o_ref, m_ref):
    for n in range(D_MODEL // MXU_COLS):
        cols = slice(n * MXU_COLS, (n + 1) * MXU_COLS)
        ya = _dot(a_ref[...], wa_ref[:, cols])
        yb = _dot(b_ref[...], wb_ref[:, cols])
        m_ref[:, cols] = (ga_ref[:, cols].astype(F32) * ya
                          + gb_ref[:, cols].astype(F32) * yb).astype(BF16)
    for n in range(D_MODEL // MXU_COLS):
        cols = slice(n * MXU_COLS, (n + 1) * MXU_COLS)
        o_ref[:, cols] = x_ref[:, cols] + _dot(m_ref[...], wo_ref[:, cols])


def _merge(oa, ob, p16, x2d, wa, wb, wo, *, tm):
    t = x2d.shape[0]
    row = lambda i: (i, 0)
    fixed = lambda i: (0, 0)
    gate0 = (SEG_GATE - N32_SEGS) * WIDTH // D_MODEL
    return pl.pallas_call(
        _merge_kernel,
        grid=(t // tm,),
        in_specs=[
            pl.BlockSpec((tm, WIDTH), row),
            pl.BlockSpec((tm, WIDTH), row),
            pl.BlockSpec((tm, D_MODEL), lambda i: (i, gate0)),
            pl.BlockSpec((tm, D_MODEL), lambda i: (i, gate0 + 1)),
            pl.BlockSpec((tm, D_MODEL), row),
            pl.BlockSpec(wa.shape, fixed),
            pl.BlockSpec(wb.shape, fixed),
            pl.BlockSpec(wo.shape, fixed),
        ],
        out_specs=pl.BlockSpec((tm, D_MODEL), row),
        out_shape=jax.ShapeDtypeStruct((t, D_MODEL), F32),
        scratch_shapes=[pltpu.VMEM((tm, D_MODEL), BF16)],
        compiler_params=_params("parallel"),
        name="merge_out",
    )(oa, ob, p16, p16, x2d, wa, wb, wo)


def _ffn_kernel(x_ref, gain_ref, w1_ref, w2_ref, o_ref, h_ref):
    j = pl.program_id(1)

    @pl.when(j == 0)
    def _():
        x = x_ref[...]
        h_ref[...] = (_rms(x) * gain_ref[...]).astype(BF16)
        o_ref[...] = x

    a = jnp.maximum(_dot(h_ref[...], w1_ref[...]), 0.0)
    o_ref[...] += _dot((a * a).astype(BF16), w2_ref[...])


def _ffn(x2d, gain, w1, w2, *, tm, tf):
    t = x2d.shape[0]
    row = lambda i, j: (i, 0)
    return pl.pallas_call(
        _ffn_kernel,
        grid=(t // tm, D_FF // tf),
        in_specs=[
            pl.BlockSpec((tm, D_MODEL), row),
            pl.BlockSpec((1, D_MODEL), lambda i, j: (0, 0)),
            pl.BlockSpec((D_MODEL, tf), lambda i, j: (0, j)),
            pl.BlockSpec((tf, D_MODEL), lambda i, j: (j, 0)),
        ],
        out_specs=pl.BlockSpec((tm, D_MODEL), row),
        out_shape=jax.ShapeDtypeStruct((t, D_MODEL), F32),
        scratch_shapes=[pltpu.VMEM((tm, D_MODEL), BF16)],
        compiler_params=_params("parallel", "arbitrary"),
        name="ffn",
    )(x2d, gain, w1, w2)


def _ple_kernel(x_ref, p_ref, gain_ref, wg_ref, wp_ref, gfin_ref, o_ref, h_ref):
    h_ref[...] = (_rms(x_ref[...]) * gain_ref[...]).astype(BF16)
    pb = p_ref[...].astype(BF16)
    sumsq = jnp.zeros((x_ref.shape[0], 1), F32)
    for n in range(D_MODEL // MXU_COLS):
        cols = slice(n * MXU_COLS, (n + 1) * MXU_COLS)
        gate = _sigmoid(_dot(h_ref[...], wg_ref[:, cols]))
        y = x_ref[:, cols] + gate * _dot(pb, wp_ref[:, cols])
        o_ref[:, cols] = y
        sumsq = sumsq + jnp.sum(y * y, axis=-1, keepdims=True)
    scale = lax.rsqrt(sumsq * (1.0 / D_MODEL) + EPS)
    o_ref[...] = o_ref[...] * scale * gfin_ref[...]


def _ple(x2d, p2d, gain, wg, wp, gfin, *, tm):
    t = x2d.shape[0]
    row = lambda i: (i, 0)
    fixed = lambda i: (0, 0)
    return pl.pallas_call(
        _ple_kernel,
        grid=(t // tm,),
        in_specs=[
            pl.BlockSpec((tm, D_MODEL), row),
            pl.BlockSpec((tm, PLE_DIM), row),
            pl.BlockSpec((1, D_MODEL), fixed),
            pl.BlockSpec(wg.shape, fixed),
            pl.BlockSpec(wp.shape, fixed),
            pl.BlockSpec((1, D_MODEL), fixed),
        ],
        out_specs=pl.BlockSpec((tm, D_MODEL), row),
        out_shape=jax.ShapeDtypeStruct((t, D_MODEL), F32),
        scratch_shapes=[pltpu.VMEM((tm, D_MODEL), BF16)],
        compiler_params=_params("parallel"),
        name="ple_final",
    )(x2d, p2d, gain, wg, wp, gfin)


def kernel(x, p, norm_mix, w_in, lb_logits, hgrn_norm, w_a_out, gmlp_ln_g, gmlp_ln_b,
           w_spatial, b_spatial, w_b_out, w_o, norm_ffn, w_ff1, w_ff2, norm_ple,
           w_ple_gate, w_ple_proj, norm_final):
    batch, seq, _ = x.shape
    assert w_in.shape[0] == 1 and seq % GMLP_CHUNK == 0
    t = batch * seq
    x2d = x.reshape(t, D_MODEL)
    row = lambda a: a.reshape(1, -1)

    p32, p16 = _inproj(x2d, row(norm_mix[0]), w_in[0], lb_logits,
                       row(gmlp_ln_g[0]), row(gmlp_ln_b[0]), tm=1024)
    bias_full = jnp.repeat(b_spatial[0].T, HEAD_DIM, axis=1)
    o_a, o_b, (wa, wb, wo, w1, w2, wg) = _mixers(
        p32, p16, row(hgrn_norm[0]), w_spatial[0], bias_full,
        [w_a_out[0], w_b_out[0], w_o[0], w_ff1[0], w_ff2[0], w_ple_gate[0]], batch, seq)
    x2d = _merge(o_a, o_b, p16, x2d, wa, wb, wo, tm=512)
    x2d = _ffn(x2d, row(norm_ffn[0]), w1, w2, tm=512, tf=1024)
    x2d = _ple(x2d, p[0].reshape(t, PLE_DIM), row(norm_ple[0]), wg,
               w_ple_proj[0].astype(BF16), row(norm_final), tm=512)
    return x2d.reshape(batch, seq, D_MODEL)
```

```python
import functools

import jax
import jax.numpy as jnp
import numpy as np
from jax import lax
from jax.experimental import pallas as pl
from jax.experimental.pallas import tpu as pltpu

F32 = jnp.float32
BF16 = jnp.bfloat16

D_MODEL = 2048
WIDTH = D_MODEL // 2
HEAD_DIM = 128
HEADS = WIDTH // HEAD_DIM
CHUNK = 64
GMLP_CHUNK = 128
D_FF = 4 * D_MODEL
PLE_DIM = 256
EPS = 1e-6
LOG2_E = np.float32(np.log2(np.e))
N_IN = 6 * WIDTH + 2 * D_MODEL

SCAN_LEVELS = 6
VMEM_LIMIT_BYTES = 56 * 1024 * 1024
MXU_COLS = 256
SCAN_GROUP = 4

SEG_Q, SEG_F, SEG_INP, SEG_G, SEG_U, SEG_V, SEG_GATE = 0, 1, 2, 3, 4, 5, 6
N32_SEGS = 2


def _rms(x):
    return x * lax.rsqrt(jnp.mean(x * x, axis=-1, keepdims=True) + EPS)


def _dot(a, b):
    return jnp.dot(a, b, preferred_element_type=F32)


def _dot_nt(a, b):
    return lax.dot_general(a, b, (((1,), (1,)), ((), ())), preferred_element_type=F32)


def _dot_tn(a, b):
    return lax.dot_general(a, b, (((0,), (0,)), ((), ())), preferred_element_type=F32)


def _gelu(x):
    return x * (lax.erf(x * np.float32(1.0 / np.sqrt(2.0))) + 1.0) * 0.5


def _sigmoid(x):
    return 0.5 * jnp.tanh(0.5 * x) + 0.5


def _params(*semantics):
    return pltpu.CompilerParams(
        dimension_semantics=semantics, vmem_limit_bytes=VMEM_LIMIT_BYTES)


def _inproj_kernel(x_ref, gain_ref, w_ref, lbl_ref, lng_ref, lnb_ref, o32_ref, o16_ref, h_ref):
    j = pl.program_id(1)

    @pl.when(j == 0)
    def _():
        h_ref[...] = (_rms(x_ref[...]) * gain_ref[...]).astype(BF16)

    def emit(o_ref, act):
        for n in range(WIDTH // MXU_COLS):
            cols = slice(n * MXU_COLS, (n + 1) * MXU_COLS)
            o_ref[:, cols] = act(_dot(h_ref[...], w_ref[:, cols].astype(BF16)), cols).astype(o_ref.dtype)

    silu = lambda a, cols: a * _sigmoid(a)

    @pl.when(j == SEG_Q)
    def _():
        emit(o32_ref, silu)

    @pl.when(j == SEG_F)
    def _():
        logits = lbl_ref[...]
        e = jnp.exp(logits - jnp.max(logits, axis=0, keepdims=True))
        lb = (e / jnp.sum(e, axis=0, keepdims=True))[0:1, :]
        emit(o32_ref, lambda a, cols: LOG2_E * jnp.log(
            lb[:, cols] + (1.0 - lb[:, cols]) * jax.nn.sigmoid(a)))

    @pl.when(j == SEG_INP)
    def _():
        emit(o16_ref, lambda a, cols: a)

    @pl.when(j == SEG_G)
    def _():
        emit(o16_ref, silu)

    @pl.when(j == SEG_U)
    def _():
        emit(o16_ref, lambda a, cols: _gelu(a))

    @pl.when(j == SEG_V)
    def _():
        a = _gelu(_dot(h_ref[...], w_ref[...].astype(BF16)))
        mu = jnp.mean(a, axis=-1, keepdims=True)
        c = a - mu
        var = jnp.mean(c * c, axis=-1, keepdims=True)
        o16_ref[...] = (c * lax.rsqrt(var + EPS) * lng_ref[...] + lnb_ref[...]).astype(BF16)

    @pl.when(j >= SEG_GATE)
    def _():
        emit(o16_ref, lambda a, cols: _sigmoid(a))


def _inproj(x2d, gain, w, lb_logits, ln_g, ln_b, *, tm):
    t = x2d.shape[0]
    row = lambda i, j: (i, 0)
    fixed = lambda i, j: (0, 0)
    n32 = N32_SEGS * WIDTH
    return pl.pallas_call(
        _inproj_kernel,
        grid=(t // tm, N_IN // WIDTH),
        in_specs=[
            pl.BlockSpec((tm, D_MODEL), row),
            pl.BlockSpec((1, D_MODEL), fixed),
            pl.BlockSpec((D_MODEL, WIDTH), lambda i, j: (0, j)),
            pl.BlockSpec(lb_logits.shape, fixed),
            pl.BlockSpec((1, WIDTH), fixed),
            pl.BlockSpec((1, WIDTH), fixed),
        ],
        out_specs=[
            pl.BlockSpec((tm, WIDTH), lambda i, j: (i, jnp.minimum(j, N32_SEGS - 1))),
            pl.BlockSpec((tm, WIDTH), lambda i, j: (i, jnp.maximum(j - N32_SEGS, 0))),
        ],
        out_shape=[jax.ShapeDtypeStruct((t, n32), F32),
                   jax.ShapeDtypeStruct((t, N_IN - n32), BF16)],
        scratch_shapes=[pltpu.VMEM((tm, D_MODEL), BF16)],
        compiler_params=_params("parallel", "arbitrary"),
        name="inproj",
    )(x2d, gain, w, lb_logits, ln_g, ln_b)


def _pair_masks():
    t = np.arange(CHUNK)[:, None]
    s = np.arange(CHUNK)[None, :]
    masks = []
    for lvl in range(SCAN_LEVELS):
        same = (t >> (lvl + 1)) == (s >> (lvl + 1))
        masks.append(same & (((t >> lvl) & 1) == 1) & (((s >> lvl) & 1) == 0))
    masks.append(t == s)
    return np.stack(masks).astype(np.float32)


def _level_exponents(p, lvl):
    d = p.shape[-1]
    half = 1 << lvl
    if half >= 8:
        p4 = p.reshape(CHUNK // (2 * half), 2, half, d)
        lo, up = p4[:, 0], p4[:, 1]
        ref = jnp.broadcast_to(lo[:, half - 1:half, :], lo.shape)
        e = jnp.stack([ref - lo, up], axis=1).reshape(CHUNK, d)
        return e, jnp.stack([lo, up + ref], axis=1).reshape(CHUNK, d)
    sub = lax.broadcasted_iota(jnp.int32, (CHUNK, d), 0)
    upper = ((sub >> lvl) & 1) == 1
    prev1 = pltpu.roll(p, 1, axis=0)
    if lvl == 0:
        return jnp.where(upper, p, 0.0), p + jnp.where(upper, prev1, 0.0)
    if lvl == 1:
        pos = sub & 3
        nxt1 = pltpu.roll(p, CHUNK - 1, axis=0)
        prev2 = pltpu.roll(p, 2, axis=0)
        ref = jnp.where(pos == 0, nxt1, jnp.where(pos == 1, p, jnp.where(pos == 2, prev1, prev2)))
    else:
        p3 = p.reshape(CHUNK // 8, 8, d)
        ref = jnp.broadcast_to(p3[:, 3:4, :], p3.shape).reshape(CHUNK, d)
    return jnp.where(upper, p, ref - p), p + jnp.where(upper, ref, 0.0)


def _mixers_kernel(n_side, q_ref, lf_ref, v_ref, g_ref, mask_ref, gn_ref,
                   u_ref, gv_ref, ws_ref, bias_ref, *refs):
    for src_ref, dst_ref in zip(refs[:n_side], refs[n_side + 2:2 * n_side + 2]):
        dst_ref[...] = src_ref[...].astype(BF16)
    o_ref, ob_ref = refs[n_side], refs[n_side + 1]
    qs_ref, ks_ref, dec_ref, at_ref, st_ref = refs[2 * n_side + 2:]
    _gmlp_rows(u_ref, gv_ref, ws_ref, bias_ref, ob_ref)
    n_groups = q_ref.shape[0] // (CHUNK * SCAN_GROUP)
    gn = gn_ref[...]

    def chunk_rows(c):
        start = c * CHUNK
        return pl.ds(start if isinstance(c, int) else pl.multiple_of(start, CHUNK), CHUNK)

    def scale(c):
        rows = chunk_rows(c)
        lf = lf_ref[rows, :]
        qb = q_ref[rows, :].astype(BF16)
        kb = (1.0 - jnp.exp2(lf)).astype(BF16)
        qs_ref[0, rows, :] = qb
        ks_ref[0, rows, :] = kb
        p = lf
        for lvl in range(SCAN_LEVELS):
            e, p = _level_exponents(p, lvl)
            wb = jnp.exp2(e).astype(BF16)
            qs_ref[1 + lvl, rows, :] = qb * wb
            if lvl > 0:
                ks_ref[lvl, rows, :] = kb * wb
        last = p[CHUNK - 1:CHUNK, :]
        qs_ref[1 + SCAN_LEVELS, rows, :] = qb * jnp.exp2(p).astype(BF16)
        ks_ref[SCAN_LEVELS, rows, :] = kb * jnp.exp2(last - p).astype(BF16)
        dec_ref[c] = jnp.broadcast_to(jnp.exp2(last), (8, HEAD_DIM))

    def scores(c, st):
        rows = chunk_rows(c)
        s_diag = _dot_nt(qs_ref[0, rows, :], ks_ref[0, rows, :])
        s_lvl = [_dot_nt(qs_ref[1 + lvl, rows, :], ks_ref[lvl, rows, :])
                 for lvl in range(SCAN_LEVELS)]
        tiles = []
        for r in range(CHUNK // 8):
            sl = slice(8 * r, 8 * r + 8)
            acc = mask_ref[SCAN_LEVELS, sl, :] * s_diag[sl]
            for lvl in range(SCAN_LEVELS):
                if lvl < 3 or (r >> (lvl - 3)) & 1:
                    acc = acc + mask_ref[lvl, sl, :] * s_lvl[lvl][sl]
            tiles.append(acc)
        at_ref[rows, :] = jnp.concatenate(tiles, axis=0).astype(BF16)
        st_ref[c] = st.astype(BF16)
        return st * dec_ref[c][0:1, :] + _dot_tn(v_ref[rows, :], ks_ref[SCAN_LEVELS, rows, :])

    def output(c):
        rows = chunk_rows(c)
        o = _dot_nt(qs_ref[1 + SCAN_LEVELS, rows, :], st_ref[c])
        o = o + _dot(at_ref[rows, :], v_ref[rows, :])
        o_ref[rows, :] = (_rms(o) * gn * g_ref[rows, :].astype(F32)).astype(BF16)

    def stage(group, st, do_output, do_scores, do_scale):
        for i in range(SCAN_GROUP):
            if do_output:
                output((group - 2) * SCAN_GROUP + i)
        for i in range(SCAN_GROUP):
            if do_scores:
                st = scores((group - 1) * SCAN_GROUP + i, st)
        for i in range(SCAN_GROUP):
            if do_scale:
                scale(group * SCAN_GROUP + i)
        return st

    st = jnp.zeros((HEAD_DIM, HEAD_DIM), F32)
    st = stage(0, st, False, False, True)
    st = stage(1, st, False, True, True)
    st = lax.fori_loop(2, n_groups, lambda g, s: stage(g, s, True, True, True), st)
    st = stage(n_groups, st, True, True, False)
    stage(n_groups + 1, st, True, False, False)


def _mixers(p32, p16, gn, w_spatial, bias_full, side_weights, batch, seq):
    masks = _pair_masks()
    steps = batch * HEADS
    t = batch * seq
    blk = lambda seg: pl.BlockSpec((seq, HEAD_DIM), lambda b, h: (b, seg * HEADS + h))
    step_rows = lambda b, h: (b * HEADS + h, 0)
    gmlp_blk = lambda seg: pl.BlockSpec((t // steps, WIDTH), lambda b, h: (b * HEADS + h, seg))
    side_specs = [pl.BlockSpec((w.shape[0] // steps, w.shape[1]), step_rows)
                  for w in side_weights]
    outs = pl.pallas_call(
        functools.partial(_mixers_kernel, len(side_weights)),
        grid=(batch, HEADS),
        in_specs=[
            blk(SEG_Q), blk(SEG_F), blk(SEG_INP - N32_SEGS), blk(SEG_G - N32_SEGS),
            pl.BlockSpec(masks.shape, lambda b, h: (0, 0, 0)),
            pl.BlockSpec((1, HEAD_DIM), lambda b, h: (0, 0)),
            gmlp_blk(SEG_U - N32_SEGS), gmlp_blk(SEG_V - N32_SEGS),
            pl.BlockSpec(w_spatial.shape, lambda b, h: (0, 0, 0)),
            pl.BlockSpec(bias_full.shape, lambda b, h: (0, 0)),
        ] + side_specs,
        out_specs=[pl.BlockSpec((seq, HEAD_DIM), lambda b, h: (b, h)),
                   pl.BlockSpec((t // steps, WIDTH), step_rows)] + side_specs,
        out_shape=[jax.ShapeDtypeStruct((t, WIDTH), BF16), jax.ShapeDtypeStruct((t, WIDTH), BF16)]
        + [jax.ShapeDtypeStruct(w.shape, BF16) for w in side_weights],
        scratch_shapes=[
            pltpu.VMEM((SCAN_LEVELS + 2, seq, HEAD_DIM), BF16),
            pltpu.VMEM((SCAN_LEVELS + 1, seq, HEAD_DIM), BF16),
            pltpu.VMEM((seq // CHUNK, 8, HEAD_DIM), F32),
            pltpu.VMEM((seq, CHUNK), BF16),
            pltpu.VMEM((seq // CHUNK, HEAD_DIM, HEAD_DIM), BF16),
        ],
        compiler_params=_params("parallel", "arbitrary"),
        name="mixers",
    )(p32, p32, p16, p16, jnp.asarray(masks), gn, p16, p16, w_spatial, bias_full, *side_weights)
    return outs[0], outs[1], outs[2:]


def _gmlp_rows(u_ref, v_ref, w_ref, bias_ref, o_ref):
    t = lax.broadcasted_iota(jnp.int32, (GMLP_CHUNK, GMLP_CHUNK), 0)
    s = lax.broadcasted_iota(jnp.int32, (GMLP_CHUNK, GMLP_CHUNK), 1)
    keep = (s // CHUNK) <= (t // CHUNK)
    for h in range(HEADS):
        cols = slice(h * HEAD_DIM, (h + 1) * HEAD_DIM)
        w = jnp.where(keep, w_ref[h], 0.0).astype(BF16)
        bias = bias_ref[:, cols]
        for g in range(u_ref.shape[0] // GMLP_CHUNK):
            rows = slice(g * GMLP_CHUNK, (g + 1) * GMLP_CHUNK)
            sv = _dot(w, v_ref[rows, cols]) + bias
            o_ref[rows, cols] = (u_ref[rows, cols].astype(F32) * sv).astype(BF16)


def _merge_kernel(a_ref, b_ref, ga_ref, gb_ref, x_ref, wa_ref, wb_ref, wo_ref, o_ref, m_ref):
    for n in range(D_MODEL // MXU_COLS):
        cols = slice(n * MXU_COLS, (n + 1) * MXU_COLS)
        ya = _dot(a_ref[...], wa_ref[:, cols])
        yb = _dot(b_ref[...], wb_ref[:, cols])
        m_ref[:, cols] = (ga_ref[:, cols].astype(F32) * ya
                          + gb_ref[:, cols].astype(F32) * yb).astype(BF16)
    for n in range(D_MODEL // MXU_COLS):
        cols = slice(n * MXU_COLS, (n + 1) * MXU_COLS)
        o_ref[:, cols] = x_ref[:, cols] + _dot(m_ref[...], wo_ref[:, cols])


def _merge(oa, ob, p16, x2d, wa, wb, wo, *, tm):
    t = x2d.shape[0]
    row = lambda i: (i, 0)
    fixed = lambda i: (0, 0)
    gate0 = (SEG_GATE - N32_SEGS) * WIDTH // D_MODEL
    return pl.pallas_call(
        _merge_kernel,
        grid=(t // tm,),
        in_specs=[
            pl.BlockSpec((tm, WIDTH), row),
            pl.BlockSpec((tm, WIDTH), row),
            pl.BlockSpec((tm, D_MODEL), lambda i: (i, gate0)),
            pl.BlockSpec((tm, D_MODEL), lambda i: (i, gate0 + 1)),
            pl.BlockSpec((tm, D_MODEL), row),
            pl.BlockSpec(wa.shape, fixed),
            pl.BlockSpec(wb.shape, fixed),
            pl.BlockSpec(wo.shape, fixed),
        ],
        out_specs=pl.BlockSpec((tm, D_MODEL), row),
        out_shape=jax.ShapeDtypeStruct((t, D_MODEL), F32),
        scratch_shapes=[pltpu.VMEM((tm, D_MODEL), BF16)],
        compiler_params=_params("parallel"),
        name="merge_out",
    )(oa, ob, p16, p16, x2d, wa, wb, wo)


def _ffn_kernel(x_ref, gain_ref, w1_ref, w2_ref, o_ref, h_ref):
    j = pl.program_id(1)

    @pl.when(j == 0)
    def _():
        x = x_ref[...]
        h_ref[...] = (_rms(x) * gain_ref[...]).astype(BF16)
        o_ref[...] = x

    a = jnp.maximum(_dot(h_ref[...], w1_ref[...]), 0.0)
    o_ref[...] += _dot((a * a).astype(BF16), w2_ref[...])


def _ffn(x2d, gain, w1, w2, *, tm, tf):
    t = x2d.shape[0]
    row = lambda i, j: (i, 0)
    return pl.pallas_call(
        _ffn_kernel,
        grid=(t // tm, D_FF // tf),
        in_specs=[
            pl.BlockSpec((tm, D_MODEL), row),
            pl.BlockSpec((1, D_MODEL), lambda i, j: (0, 0)),
            pl.BlockSpec((D_MODEL, tf), lambda i, j: (0, j)),
            pl.BlockSpec((tf, D_MODEL), lambda i, j: (j, 0)),
        ],
        out_specs=pl.BlockSpec((tm, D_MODEL), row),
        out_shape=jax.ShapeDtypeStruct((t, D_MODEL), F32),
        scratch_shapes=[pltpu.VMEM((tm, D_MODEL), BF16)],
        compiler_params=_params("parallel", "arbitrary"),
        name="ffn",
    )(x2d, gain, w1, w2)


def _ple_kernel(x_ref, p_ref, gain_ref, wg_ref, wp_ref, gfin_ref, o_ref, h_ref):
    h_ref[...] = (_rms(x_ref[...]) * gain_ref[...]).astype(BF16)
    pb = p_ref[...].astype(BF16)
    sumsq = jnp.zeros((x_ref.shape[0], 1), F32)
    for n in range(D_MODEL // MXU_COLS):
        cols = slice(n * MXU_COLS, (n + 1) * MXU_COLS)
        gate = _sigmoid(_dot(h_ref[...], wg_ref[:, cols]))
        y = x_ref[:, cols] + gate * _dot(pb, wp_ref[:, cols])
        o_ref[:, cols] = y
        sumsq = sumsq + jnp.sum(y * y, axis=-1, keepdims=True)
    scale = lax.rsqrt(sumsq * (1.0 / D_MODEL) + EPS)
    o_ref[...] = o_ref[...] * scale * gfin_ref[...]


def _ple(x2d, p2d, gain, wg, wp, gfin, *, tm):
    t = x2d.shape[0]
    row = lambda i: (i, 0)
    fixed = lambda i: (0, 0)
    return pl.pallas_call(
        _ple_kernel,
        grid=(t // tm,),
        in_specs=[
            pl.BlockSpec((tm, D_MODEL), row),
            pl.BlockSpec((tm, PLE_DIM), row),
            pl.BlockSpec((1, D_MODEL), fixed),
            pl.BlockSpec(wg.shape, fixed),
            pl.BlockSpec(wp.shape, fixed),
            pl.BlockSpec((1, D_MODEL), fixed),
        ],
        out_specs=pl.BlockSpec((tm, D_MODEL), row),
        out_shape=jax.ShapeDtypeStruct((t, D_MODEL), F32),
        scratch_shapes=[pltpu.VMEM((tm, D_MODEL), BF16)],
        compiler_params=_params("parallel"),
        name="ple_final",
    )(x2d, p2d, gain, wg, wp, gfin)


def kernel(x, p, norm_mix, w_in, lb_logits, hgrn_norm, w_a_out, gmlp_ln_g, gmlp_ln_b,
           w_spatial, b_spatial, w_b_out, w_o, norm_ffn, w_ff1, w_ff2, norm_ple,
           w_ple_gate, w_ple_proj, norm_final):
    batch, seq, _ = x.shape
    assert w_in.shape[0] == 1 and seq % GMLP_CHUNK == 0
    t = batch * seq
    x2d = x.reshape(t, D_MODEL)
    row = lambda a: a.reshape(1, -1)

    p32, p16 = _inproj(x2d, row(norm_mix[0]), w_in[0], lb_logits,
                       row(gmlp_ln_g[0]), row(gmlp_ln_b[0]), tm=1024)
    bias_full = jnp.repeat(b_spatial[0].T, HEAD_DIM, axis=1)
    o_a, o_b, (wa, wb, wo, w1, w2, wg) = _mixers(
        p32, p16, row(hgrn_norm[0]), w_spatial[0], bias_full,
        [w_a_out[0], w_b_out[0], w_o[0], w_ff1[0], w_ff2[0], w_ple_gate[0]], batch, seq)
    x2d = _merge(o_a, o_b, p16, x2d, wa, wb, wo, tm=512)
    x2d = _ffn(x2d, row(norm_ffn[0]), w1, w2, tm=1024, tf=512)
    x2d = _ple(x2d, p[0].reshape(t, PLE_DIM), row(norm_ple[0]), wg,
               w_ple_proj[0].astype(BF16), row(norm_final), tm=512)
    return x2d.reshape(batch, seq, D_MODEL)
```

```python
import functools

import jax
import jax.numpy as jnp
import numpy as np
from jax import lax
from jax.experimental import pallas as pl
from jax.experimental.pallas import tpu as pltpu

F32 = jnp.float32
BF16 = jnp.bfloat16

D_MODEL = 2048
WIDTH = D_MODEL // 2
HEAD_DIM = 128
HEADS = WIDTH // HEAD_DIM
CHUNK = 64
GMLP_CHUNK = 128
D_FF = 4 * D_MODEL
PLE_DIM = 256
EPS = 1e-6
LOG2_E = np.float32(np.log2(np.e))
N_IN = 6 * WIDTH + 2 * D_MODEL

SCAN_LEVELS = 6
VMEM_LIMIT_BYTES = 56 * 1024 * 1024
MXU_COLS = 256
SCAN_GROUP = 4

SEG_Q, SEG_F, SEG_INP, SEG_G, SEG_U, SEG_V, SEG_GATE = 0, 1, 2, 3, 4, 5, 6
N32_SEGS = 2


def _rms(x):
    return x * lax.rsqrt(jnp.mean(x * x, axis=-1, keepdims=True) + EPS)


def _gain_and_row_scale(x, gain):
    r = lax.rsqrt(jnp.mean(x * x, axis=-1, keepdims=True) + EPS)
    return (x * gain).astype(BF16), r


def _dot(a, b):
    return jnp.dot(a, b, preferred_element_type=F32)


def _dot_nt(a, b):
    return lax.dot_general(a, b, (((1,), (1,)), ((), ())), preferred_element_type=F32)


def _dot_tn(a, b):
    return lax.dot_general(a, b, (((0,), (0,)), ((), ())), preferred_element_type=F32)


def _gelu(x):
    return x * (lax.erf(x * np.float32(1.0 / np.sqrt(2.0))) + 1.0) * 0.5


def _sigmoid(x):
    return 0.5 * jnp.tanh(0.5 * x) + 0.5


def _params(*semantics):
    return pltpu.CompilerParams(
        dimension_semantics=semantics, vmem_limit_bytes=VMEM_LIMIT_BYTES)


def _inproj_kernel(x_ref, gain_ref, w_ref, lbl_ref, lng_ref, lnb_ref, o32_ref, o16_ref,
                   h_ref, r_ref):
    j = pl.program_id(1)

    def proj(cols):
        return r_ref[...] * _dot(h_ref[...], w_ref[:, cols].astype(BF16))

    def emit(o_ref, act):
        for n in range(WIDTH // MXU_COLS):
            cols = slice(n * MXU_COLS, (n + 1) * MXU_COLS)
            o_ref[:, cols] = act(proj(cols), cols).astype(o_ref.dtype)

    silu = lambda a, cols: a * _sigmoid(a)

    @pl.when(j == SEG_Q)
    def _():
        h_ref[...], r_ref[...] = _gain_and_row_scale(x_ref[...], gain_ref[...])
        emit(o32_ref, silu)

    @pl.when(j == SEG_F)
    def _():
        logits = lbl_ref[...]
        e = jnp.exp(logits - jnp.max(logits, axis=0, keepdims=True))
        lb = (e / jnp.sum(e, axis=0, keepdims=True))[0:1, :]
        emit(o32_ref, lambda a, cols: LOG2_E * jnp.log(
            lb[:, cols] + (1.0 - lb[:, cols]) * jax.nn.sigmoid(a)))

    @pl.when(j == SEG_INP)
    def _():
        emit(o16_ref, lambda a, cols: a)

    @pl.when(j == SEG_G)
    def _():
        emit(o16_ref, silu)

    @pl.when(j == SEG_U)
    def _():
        emit(o16_ref, lambda a, cols: _gelu(a))

    @pl.when(j == SEG_V)
    def _():
        a = _gelu(proj(slice(None)))
        mu = jnp.mean(a, axis=-1, keepdims=True)
        c = a - mu
        var = jnp.mean(c * c, axis=-1, keepdims=True)
        o16_ref[...] = (c * lax.rsqrt(var + EPS) * lng_ref[...] + lnb_ref[...]).astype(BF16)

    @pl.when(j >= SEG_GATE)
    def _():
        emit(o16_ref, lambda a, cols: _sigmoid(a))


def _inproj(x2d, gain, w, lb_logits, ln_g, ln_b, *, tm):
    t = x2d.shape[0]
    row = lambda i, j: (i, 0)
    fixed = lambda i, j: (0, 0)
    n32 = N32_SEGS * WIDTH
    return pl.pallas_call(
        _inproj_kernel,
        grid=(t // tm, N_IN // WIDTH),
        in_specs=[
            pl.BlockSpec((tm, D_MODEL), row),
            pl.BlockSpec((1, D_MODEL), fixed),
            pl.BlockSpec((D_MODEL, WIDTH), lambda i, j: (0, j)),
            pl.BlockSpec(lb_logits.shape, fixed),
            pl.BlockSpec((1, WIDTH), fixed),
            pl.BlockSpec((1, WIDTH), fixed),
        ],
        out_specs=[
            pl.BlockSpec((tm, WIDTH), lambda i, j: (i, jnp.minimum(j, N32_SEGS - 1))),
            pl.BlockSpec((tm, WIDTH), lambda i, j: (i, jnp.maximum(j - N32_SEGS, 0))),
        ],
        out_shape=[jax.ShapeDtypeStruct((t, n32), F32),
                   jax.ShapeDtypeStruct((t, N_IN - n32), BF16)],
        scratch_shapes=[pltpu.VMEM((tm, D_MODEL), BF16), pltpu.VMEM((tm, 1), F32)],
        compiler_params=_params("parallel", "arbitrary"),
        name="inproj",
    )(x2d, gain, w, lb_logits, ln_g, ln_b)


def _pair_masks():
    t = np.arange(CHUNK)[:, None]
    s = np.arange(CHUNK)[None, :]
    masks = []
    for lvl in range(SCAN_LEVELS):
        same = (t >> (lvl + 1)) == (s >> (lvl + 1))
        masks.append(same & (((t >> lvl) & 1) == 1) & (((s >> lvl) & 1) == 0))
    masks.append(t == s)
    return np.stack(masks).astype(np.float32)


def _level_exponents(p, lvl):
    d = p.shape[-1]
    half = 1 << lvl
    if half >= 8:
        p4 = p.reshape(CHUNK // (2 * half), 2, half, d)
        lo, up = p4[:, 0], p4[:, 1]
        ref = jnp.broadcast_to(lo[:, half - 1:half, :], lo.shape)
        e = jnp.stack([ref - lo, up], axis=1).reshape(CHUNK, d)
        return e, jnp.stack([lo, up + ref], axis=1).reshape(CHUNK, d)
    sub = lax.broadcasted_iota(jnp.int32, (CHUNK, d), 0)
    upper = ((sub >> lvl) & 1) == 1
    prev1 = pltpu.roll(p, 1, axis=0)
    if lvl == 0:
        return jnp.where(upper, p, 0.0), p + jnp.where(upper, prev1, 0.0)
    if lvl == 1:
        pos = sub & 3
        nxt1 = pltpu.roll(p, CHUNK - 1, axis=0)
        prev2 = pltpu.roll(p, 2, axis=0)
        ref = jnp.where(pos == 0, nxt1, jnp.where(pos == 1, p, jnp.where(pos == 2, prev1, prev2)))
    else:
        p3 = p.reshape(CHUNK // 8, 8, d)
        ref = jnp.broadcast_to(p3[:, 3:4, :], p3.shape).reshape(CHUNK, d)
    return jnp.where(upper, p, ref - p), p + jnp.where(upper, ref, 0.0)


def _mixers_kernel(n_side, q_ref, lf_ref, v_ref, g_ref, mask_ref, gn_ref,
                   u_ref, gv_ref, ws_ref, bias_ref, *refs):
    for src_ref, dst_ref in zip(refs[:n_side], refs[n_side + 2:2 * n_side + 2]):
        dst_ref[...] = src_ref[...].astype(BF16)
    o_ref, ob_ref = refs[n_side], refs[n_side + 1]
    qs_ref, ks_ref, dec_ref, at_ref, st_ref = refs[2 * n_side + 2:]
    _gmlp_rows(u_ref, gv_ref, ws_ref, bias_ref, ob_ref)
    n_groups = q_ref.shape[0] // (CHUNK * SCAN_GROUP)
    gn = gn_ref[...]

    def chunk_rows(c):
        start = c * CHUNK
        return pl.ds(start if isinstance(c, int) else pl.multiple_of(start, CHUNK), CHUNK)

    def scale(c):
        rows = chunk_rows(c)
        lf = lf_ref[rows, :]
        qb = q_ref[rows, :].astype(BF16)
        kb = (1.0 - jnp.exp2(lf)).astype(BF16)
        qs_ref[0, rows, :] = qb
        ks_ref[0, rows, :] = kb
        p = lf
        for lvl in range(SCAN_LEVELS):
            e, p = _level_exponents(p, lvl)
            wb = jnp.exp2(e).astype(BF16)
            qs_ref[1 + lvl, rows, :] = qb * wb
            if lvl > 0:
                ks_ref[lvl, rows, :] = kb * wb
        last = p[CHUNK - 1:CHUNK, :]
        qs_ref[1 + SCAN_LEVELS, rows, :] = qb * jnp.exp2(p).astype(BF16)
        ks_ref[SCAN_LEVELS, rows, :] = kb * jnp.exp2(last - p).astype(BF16)
        dec_ref[c] = jnp.broadcast_to(jnp.exp2(last), (8, HEAD_DIM))

    def scores(c, st):
        rows = chunk_rows(c)
        s_diag = _dot_nt(qs_ref[0, rows, :], ks_ref[0, rows, :])
        s_lvl = [_dot_nt(qs_ref[1 + lvl, rows, :], ks_ref[lvl, rows, :])
                 for lvl in range(SCAN_LEVELS)]
        tiles = []
        for r in range(CHUNK // 8):
            sl = slice(8 * r, 8 * r + 8)
            acc = mask_ref[SCAN_LEVELS, sl, :] * s_diag[sl]
            for lvl in range(SCAN_LEVELS):
                if lvl < 3 or (r >> (lvl - 3)) & 1:
                    acc = acc + mask_ref[lvl, sl, :] * s_lvl[lvl][sl]
            tiles.append(acc)
        at_ref[rows, :] = jnp.concatenate(tiles, axis=0).astype(BF16)
        st_ref[c] = st.astype(BF16)
        return st * dec_ref[c][0:1, :] + _dot_tn(v_ref[rows, :], ks_ref[SCAN_LEVELS, rows, :])

    def output(c):
        rows = chunk_rows(c)
        o = _dot_nt(qs_ref[1 + SCAN_LEVELS, rows, :], st_ref[c])
        o = o + _dot(at_ref[rows, :], v_ref[rows, :])
        o_ref[rows, :] = (_rms(o) * gn * g_ref[rows, :].astype(F32)).astype(BF16)

    def stage(group, st, do_output, do_scores, do_scale):
        for i in range(SCAN_GROUP):
            if do_output:
                output((group - 2) * SCAN_GROUP + i)
        for i in range(SCAN_GROUP):
            if do_scores:
                st = scores((group - 1) * SCAN_GROUP + i, st)
        for i in range(SCAN_GROUP):
            if do_scale:
                scale(group * SCAN_GROUP + i)
        return st

    st = jnp.zeros((HEAD_DIM, HEAD_DIM), F32)
    st = stage(0, st, False, False, True)
    st = stage(1, st, False, True, True)
    st = lax.fori_loop(2, n_groups, lambda g, s: stage(g, s, True, True, True), st)
    st = stage(n_groups, st, True, True, False)
    stage(n_groups + 1, st, True, False, False)


def _mixers(p32, p16, gn, w_spatial, bias_full, side_weights, batch, seq):
    masks = _pair_masks()
    steps = batch * HEADS
    t = batch * seq
    blk = lambda seg: pl.BlockSpec((seq, HEAD_DIM), lambda b, h: (b, seg * HEADS + h))
    step_rows = lambda b, h: (b * HEADS + h, 0)
    gmlp_blk = lambda seg: pl.BlockSpec((t // steps, WIDTH), lambda b, h: (b * HEADS + h, seg))
    side_specs = [pl.BlockSpec((w.shape[0] // steps, w.shape[1]), step_rows)
                  for w in side_weights]
    outs = pl.pallas_call(
        functools.partial(_mixers_kernel, len(side_weights)),
        grid=(batch, HEADS),
        in_specs=[
            blk(SEG_Q), blk(SEG_F), blk(SEG_INP - N32_SEGS), blk(SEG_G - N32_SEGS),
            pl.BlockSpec(masks.shape, lambda b, h: (0, 0, 0)),
            pl.BlockSpec((1, HEAD_DIM), lambda b, h: (0, 0)),
            gmlp_blk(SEG_U - N32_SEGS), gmlp_blk(SEG_V - N32_SEGS),
            pl.BlockSpec(w_spatial.shape, lambda b, h: (0, 0, 0)),
            pl.BlockSpec(bias_full.shape, lambda b, h: (0, 0)),
        ] + side_specs,
        out_specs=[pl.BlockSpec((seq, HEAD_DIM), lambda b, h: (b, h)),
                   pl.BlockSpec((t // steps, WIDTH), step_rows)] + side_specs,
        out_shape=[jax.ShapeDtypeStruct((t, WIDTH), BF16), jax.ShapeDtypeStruct((t, WIDTH), BF16)]
        + [jax.ShapeDtypeStruct(w.shape, BF16) for w in side_weights],
        scratch_shapes=[
            pltpu.VMEM((SCAN_LEVELS + 2, seq, HEAD_DIM), BF16),
            pltpu.VMEM((SCAN_LEVELS + 1, seq, HEAD_DIM), BF16),
            pltpu.VMEM((seq // CHUNK, 8, HEAD_DIM), F32),
            pltpu.VMEM((seq, CHUNK), BF16),
            pltpu.VMEM((seq // CHUNK, HEAD_DIM, HEAD_DIM), BF16),
        ],
        compiler_params=_params("parallel", "arbitrary"),
        name="mixers",
    )(p32, p32, p16, p16, jnp.asarray(masks), gn, p16, p16, w_spatial, bias_full, *side_weights)
    return outs[0], outs[1], outs[2:]


def _gmlp_rows(u_ref, v_ref, w_ref, bias_ref, o_ref):
    t = lax.broadcasted_iota(jnp.int32, (GMLP_CHUNK, GMLP_CHUNK), 0)
    s = lax.broadcasted_iota(jnp.int32, (GMLP_CHUNK, GMLP_CHUNK), 1)
    keep = (s // CHUNK) <= (t // CHUNK)
    for h in range(HEADS):
        cols = slice(h * HEAD_DIM, (h + 1) * HEAD_DIM)
        w = jnp.where(keep, w_ref[h], 0.0).astype(BF16)
        bias = bias_ref[:, cols]
        for g in range(u_ref.shape[0] // GMLP_CHUNK):
            rows = slice(g * GMLP_CHUNK, (g + 1) * GMLP_CHUNK)
            sv = _dot(w, v_ref[rows, cols]) + bias
            o_ref[rows, cols] = (u_ref[rows, cols].astype(F32) * sv).astype(BF16)


def _merge_kernel(a_ref, b_ref, ga_ref, gb_ref, x_ref, wa_ref, wb_ref, wo_ref, o_ref, m_ref):
    for n in range(D_MODEL // MXU_COLS):
        cols = slice(n * MXU_COLS, (n + 1) * MXU_COLS)
        ya = _dot(a_ref[...], wa_ref[:, cols])
        yb = _dot(b_ref[...], wb_ref[:, cols])
        m_ref[:, cols] = (ga_ref[:, cols].astype(F32) * ya
                          + gb_ref[:, cols].astype(F32) * yb).astype(BF16)
    for n in range(D_MODEL // MXU_COLS):
        cols = slice(n * MXU_COLS, (n + 1) * MXU_COLS)
        o_ref[:, cols] = x_ref[:, cols] + _dot(m_ref[...], wo_ref[:, cols])


def _merge(oa, ob, p16, x2d, wa, wb, wo, *, tm):
    t = x2d.shape[0]
    row = lambda i: (i, 0)
    fixed = lambda i: (0, 0)
    gate0 = (SEG_GATE - N32_SEGS) * WIDTH // D_MODEL
    return pl.pallas_call(
        _merge_kernel,
        grid=(t // tm,),
        in_specs=[
            pl.BlockSpec((tm, WIDTH), row),
            pl.BlockSpec((tm, WIDTH), row),
            pl.BlockSpec((tm, D_MODEL), lambda i: (i, gate0)),
            pl.BlockSpec((tm, D_MODEL), lambda i: (i, gate0 + 1)),
            pl.BlockSpec((tm, D_MODEL), row),
            pl.BlockSpec(wa.shape, fixed),
            pl.BlockSpec(wb.shape, fixed),
            pl.BlockSpec(wo.shape, fixed),
        ],
        out_specs=pl.BlockSpec((tm, D_MODEL), row),
        out_shape=jax.ShapeDtypeStruct((t, D_MODEL), F32),
        scratch_shapes=[pltpu.VMEM((tm, D_MODEL), BF16)],
        compiler_params=_params("parallel"),
        name="merge_out",
    )(oa, ob, p16, p16, x2d, wa, wb, wo)


def _ffn_kernel(x_ref, gain_ref, w1_ref, w2_ref, o_ref, h_ref, r_ref):
    j = pl.program_id(1)

    def hidden_tile():
        a = jnp.maximum(r_ref[...] * _dot(h_ref[...], w1_ref[...]), 0.0)
        return _dot((a * a).astype(BF16), w2_ref[...])

    @pl.when(j == 0)
    def _():
        x = x_ref[...]
        h_ref[...], r_ref[...] = _gain_and_row_scale(x, gain_ref[...])
        o_ref[...] = x + hidden_tile()

    @pl.when(j > 0)
    def _():
        o_ref[...] += hidden_tile()


def _ffn(x2d, gain, w1, w2, *, tm, tf):
    t = x2d.shape[0]
    row = lambda i, j: (i, 0)
    return pl.pallas_call(
        _ffn_kernel,
        grid=(t // tm, D_FF // tf),
        in_specs=[
            pl.BlockSpec((tm, D_MODEL), row),
            pl.BlockSpec((1, D_MODEL), lambda i, j: (0, 0)),
            pl.BlockSpec((D_MODEL, tf), lambda i, j: (0, j)),
            pl.BlockSpec((tf, D_MODEL), lambda i, j: (j, 0)),
        ],
        out_specs=pl.BlockSpec((tm, D_MODEL), row),
        out_shape=jax.ShapeDtypeStruct((t, D_MODEL), F32),
        scratch_shapes=[pltpu.VMEM((tm, D_MODEL), BF16), pltpu.VMEM((tm, 1), F32)],
        compiler_params=_params("parallel", "arbitrary"),
        name="ffn",
    )(x2d, gain, w1, w2)


def _ple_kernel(x_ref, p_ref, gain_ref, wg_ref, wp_ref, gfin_ref, o_ref, h_ref):
    h_ref[...], r = _gain_and_row_scale(x_ref[...], gain_ref[...])
    pb = p_ref[...].astype(BF16)
    sumsq = jnp.zeros((x_ref.shape[0], 1), F32)
    for n in range(D_MODEL // MXU_COLS):
        cols = slice(n * MXU_COLS, (n + 1) * MXU_COLS)
        gate = _sigmoid(r * _dot(h_ref[...], wg_ref[:, cols]))
        y = x_ref[:, cols] + gate * _dot(pb, wp_ref[:, cols])
        o_ref[:, cols] = y
        sumsq = sumsq + jnp.sum(y * y, axis=-1, keepdims=True)
    scale = lax.rsqrt(sumsq * (1.0 / D_MODEL) + EPS)
    o_ref[...] = o_ref[...] * scale * gfin_ref[...]


def _ple(x2d, p2d, gain, wg, wp, gfin, *, tm):
    t = x2d.shape[0]
    row = lambda i: (i, 0)
    fixed = lambda i: (0, 0)
    return pl.pallas_call(
        _ple_kernel,
        grid=(t // tm,),
        in_specs=[
            pl.BlockSpec((tm, D_MODEL), row),
            pl.BlockSpec((tm, PLE_DIM), row),
            pl.BlockSpec((1, D_MODEL), fixed),
            pl.BlockSpec(wg.shape, fixed),
            pl.BlockSpec(wp.shape, fixed),
            pl.BlockSpec((1, D_MODEL), fixed),
        ],
        out_specs=pl.BlockSpec((tm, D_MODEL), row),
        out_shape=jax.ShapeDtypeStruct((t, D_MODEL), F32),
        scratch_shapes=[pltpu.VMEM((tm, D_MODEL), BF16)],
        compiler_params=_params("parallel"),
        name="ple_final",
    )(x2d, p2d, gain, wg, wp, gfin)


def kernel(x, p, norm_mix, w_in, lb_logits, hgrn_norm, w_a_out, gmlp_ln_g, gmlp_ln_b,
           w_spatial, b_spatial, w_b_out, w_o, norm_ffn, w_ff1, w_ff2, norm_ple,
           w_ple_gate, w_ple_proj, norm_final):
    batch, seq, _ = x.shape
    assert w_in.shape[0] == 1 and seq % GMLP_CHUNK == 0
    t = batch * seq
    x2d = x.reshape(t, D_MODEL)
    row = lambda a: a.reshape(1, -1)

    p32, p16 = _inproj(x2d, row(norm_mix[0]), w_in[0], lb_logits,
                       row(gmlp_ln_g[0]), row(gmlp_ln_b[0]), tm=1024)
    bias_full = jnp.repeat(b_spatial[0].T, HEAD_DIM, axis=1)
    o_a, o_b, (wa, wb, wo, w1, w2, wg) = _mixers(
        p32, p16, row(hgrn_norm[0]), w_spatial[0], bias_full,
        [w_a_out[0], w_b_out[0], w_o[0], w_ff1[0], w_ff2[0], w_ple_gate[0]], batch, seq)
    x2d = _merge(o_a, o_b, p16, x2d, wa, wb, wo, tm=512)
    x2d = _ffn(x2d, row(norm_ffn[0]), w1, w2, tm=1024, tf=512)
    x2d = _ple(x2d, p[0].reshape(t, PLE_DIM), row(norm_ple[0]), wg,
               w_ple_proj[0].astype(BF16), row(norm_final), tm=512)
    return x2d.reshape(batch, seq, D_MODEL)
```

```python
import functools

import jax
import jax.numpy as jnp
import numpy as np
from jax import lax
from jax.experimental import pallas as pl
from jax.experimental.pallas import tpu as pltpu

F32 = jnp.float32
BF16 = jnp.bfloat16

D_MODEL = 2048
WIDTH = D_MODEL // 2
HEAD_DIM = 128
HEADS = WIDTH // HEAD_DIM
CHUNK = 64
GMLP_CHUNK = 128
D_FF = 4 * D_MODEL
PLE_DIM = 256
EPS = 1e-6
LOG2_E = np.float32(np.log2(np.e))
N_IN = 6 * WIDTH + 2 * D_MODEL

SCAN_LEVELS = 6
VMEM_LIMIT_BYTES = 56 * 1024 * 1024
FFN_VMEM_LIMIT_BYTES = 62 * 1024 * 1024
MXU_COLS = 256
SCAN_GROUP = 4

SEG_Q, SEG_F, SEG_INP, SEG_G, SEG_U, SEG_V, SEG_GATE = 0, 1, 2, 3, 4, 5, 6
N32_SEGS = 2


def _rms(x):
    return x * lax.rsqrt(jnp.mean(x * x, axis=-1, keepdims=True) + EPS)


def _dot(a, b):
    return jnp.dot(a, b, preferred_element_type=F32)


def _dot_nt(a, b):
    return lax.dot_general(a, b, (((1,), (1,)), ((), ())), preferred_element_type=F32)


def _dot_tn(a, b):
    return lax.dot_general(a, b, (((0,), (0,)), ((), ())), preferred_element_type=F32)


def _gelu(x):
    return x * (lax.erf(x * np.float32(1.0 / np.sqrt(2.0))) + 1.0) * 0.5


def _sigmoid(x):
    return 0.5 * jnp.tanh(0.5 * x) + 0.5


def _params(*semantics, vmem_limit_bytes=VMEM_LIMIT_BYTES):
    return pltpu.CompilerParams(
        dimension_semantics=semantics, vmem_limit_bytes=vmem_limit_bytes)


def _inproj_kernel(x_ref, gain_ref, w_ref, lbl_ref, lng_ref, lnb_ref, o32_ref, o16_ref, h_ref):
    j = pl.program_id(1)

    @pl.when(j == 0)
    def _():
        h_ref[...] = (_rms(x_ref[...]) * gain_ref[...]).astype(BF16)

    def emit(o_ref, act):
        for n in range(WIDTH // MXU_COLS):
            cols = slice(n * MXU_COLS, (n + 1) * MXU_COLS)
            o_ref[:, cols] = act(_dot(h_ref[...], w_ref[:, cols].astype(BF16)), cols).astype(o_ref.dtype)

    silu = lambda a, cols: a * _sigmoid(a)

    @pl.when(j == SEG_Q)
    def _():
        emit(o32_ref, silu)

    @pl.when(j == SEG_F)
    def _():
        logits = lbl_ref[...]
        e = jnp.exp(logits - jnp.max(logits, axis=0, keepdims=True))
        lb = (e / jnp.sum(e, axis=0, keepdims=True))[0:1, :]
        emit(o32_ref, lambda a, cols: LOG2_E * jnp.log(
            lb[:, cols] + (1.0 - lb[:, cols]) * jax.nn.sigmoid(a)))

    @pl.when(j == SEG_INP)
    def _():
        emit(o16_ref, lambda a, cols: a)

    @pl.when(j == SEG_G)
    def _():
        emit(o16_ref, silu)

    @pl.when(j == SEG_U)
    def _():
        emit(o16_ref, lambda a, cols: _gelu(a))

    @pl.when(j == SEG_V)
    def _():
        a = _gelu(_dot(h_ref[...], w_ref[...].astype(BF16)))
        mu = jnp.mean(a, axis=-1, keepdims=True)
        c = a - mu
        var = jnp.mean(c * c, axis=-1, keepdims=True)
        o16_ref[...] = (c * lax.rsqrt(var + EPS) * lng_ref[...] + lnb_ref[...]).astype(BF16)

    @pl.when(j >= SEG_GATE)
    def _():
        emit(o16_ref, lambda a, cols: _sigmoid(a))


def _inproj(x2d, gain, w, lb_logits, ln_g, ln_b, *, tm):
    t = x2d.shape[0]
    row = lambda i, j: (i, 0)
    fixed = lambda i, j: (0, 0)
    n32 = N32_SEGS * WIDTH
    return pl.pallas_call(
        _inproj_kernel,
        grid=(t // tm, N_IN // WIDTH),
        in_specs=[
            pl.BlockSpec((tm, D_MODEL), row),
            pl.BlockSpec((1, D_MODEL), fixed),
            pl.BlockSpec((D_MODEL, WIDTH), lambda i, j: (0, j)),
            pl.BlockSpec(lb_logits.shape, fixed),
            pl.BlockSpec((1, WIDTH), fixed),
            pl.BlockSpec((1, WIDTH), fixed),
        ],
        out_specs=[
            pl.BlockSpec((tm, WIDTH), lambda i, j: (i, jnp.minimum(j, N32_SEGS - 1))),
            pl.BlockSpec((tm, WIDTH), lambda i, j: (i, jnp.maximum(j - N32_SEGS, 0))),
        ],
        out_shape=[jax.ShapeDtypeStruct((t, n32), F32),
                   jax.ShapeDtypeStruct((t, N_IN - n32), BF16)],
        scratch_shapes=[pltpu.VMEM((tm, D_MODEL), BF16)],
        compiler_params=_params("parallel", "arbitrary"),
        name="inproj",
    )(x2d, gain, w, lb_logits, ln_g, ln_b)


def _pair_masks():
    t = np.arange(CHUNK)[:, None]
    s = np.arange(CHUNK)[None, :]
    masks = []
    for lvl in range(SCAN_LEVELS):
        same = (t >> (lvl + 1)) == (s >> (lvl + 1))
        masks.append(same & (((t >> lvl) & 1) == 1) & (((s >> lvl) & 1) == 0))
    masks.append(t == s)
    return np.stack(masks).astype(np.float32)


def _level_exponents(p, lvl):
    d = p.shape[-1]
    half = 1 << lvl
    if half >= 8:
        p4 = p.reshape(CHUNK // (2 * half), 2, half, d)
        lo, up = p4[:, 0], p4[:, 1]
        ref = jnp.broadcast_to(lo[:, half - 1:half, :], lo.shape)
        e = jnp.stack([ref - lo, up], axis=1).reshape(CHUNK, d)
        return e, jnp.stack([lo, up + ref], axis=1).reshape(CHUNK, d)
    sub = lax.broadcasted_iota(jnp.int32, (CHUNK, d), 0)
    upper = ((sub >> lvl) & 1) == 1
    prev1 = pltpu.roll(p, 1, axis=0)
    if lvl == 0:
        return jnp.where(upper, p, 0.0), p + jnp.where(upper, prev1, 0.0)
    if lvl == 1:
        pos = sub & 3
        nxt1 = pltpu.roll(p, CHUNK - 1, axis=0)
        prev2 = pltpu.roll(p, 2, axis=0)
        ref = jnp.where(pos == 0, nxt1, jnp.where(pos == 1, p, jnp.where(pos == 2, prev1, prev2)))
    else:
        p3 = p.reshape(CHUNK // 8, 8, d)
        ref = jnp.broadcast_to(p3[:, 3:4, :], p3.shape).reshape(CHUNK, d)
    return jnp.where(upper, p, ref - p), p + jnp.where(upper, ref, 0.0)


def _mixers_kernel(n_side, q_ref, lf_ref, v_ref, g_ref, mask_ref, gn_ref,
                   u_ref, gv_ref, ws_ref, bias_ref, *refs):
    for src_ref, dst_ref in zip(refs[:n_side], refs[n_side + 2:2 * n_side + 2]):
        dst_ref[...] = src_ref[...].astype(BF16)
    o_ref, ob_ref = refs[n_side], refs[n_side + 1]
    qs_ref, ks_ref, dec_ref, at_ref, st_ref = refs[2 * n_side + 2:]
    _gmlp_rows(u_ref, gv_ref, ws_ref, bias_ref, ob_ref)
    n_groups = q_ref.shape[0] // (CHUNK * SCAN_GROUP)
    gn = gn_ref[...]

    def chunk_rows(c):
        start = c * CHUNK
        return pl.ds(start if isinstance(c, int) else pl.multiple_of(start, CHUNK), CHUNK)

    def scale(c):
        rows = chunk_rows(c)
        lf = lf_ref[rows, :]
        qb = q_ref[rows, :].astype(BF16)
        kb = (1.0 - jnp.exp2(lf)).astype(BF16)
        qs_ref[0, rows, :] = qb
        ks_ref[0, rows, :] = kb
        p = lf
        for lvl in range(SCAN_LEVELS):
            e, p = _level_exponents(p, lvl)
            wb = jnp.exp2(e).astype(BF16)
            qs_ref[1 + lvl, rows, :] = qb * wb
            if lvl > 0:
                ks_ref[lvl, rows, :] = kb * wb
        last = p[CHUNK - 1:CHUNK, :]
        qs_ref[1 + SCAN_LEVELS, rows, :] = qb * jnp.exp2(p).astype(BF16)
        ks_ref[SCAN_LEVELS, rows, :] = kb * jnp.exp2(last - p).astype(BF16)
        dec_ref[c] = jnp.broadcast_to(jnp.exp2(last), (8, HEAD_DIM))

    def scores(c, st):
        rows = chunk_rows(c)
        s_diag = _dot_nt(qs_ref[0, rows, :], ks_ref[0, rows, :])
        s_lvl = [_dot_nt(qs_ref[1 + lvl, rows, :], ks_ref[lvl, rows, :])
                 for lvl in range(SCAN_LEVELS)]
        tiles = []
        for r in range(CHUNK // 8):
            sl = slice(8 * r, 8 * r + 8)
            acc = mask_ref[SCAN_LEVELS, sl, :] * s_diag[sl]
            for lvl in range(SCAN_LEVELS):
                if lvl < 3 or (r >> (lvl - 3)) & 1:
                    acc = acc + mask_ref[lvl, sl, :] * s_lvl[lvl][sl]
            tiles.append(acc)
        at_ref[rows, :] = jnp.concatenate(tiles, axis=0).astype(BF16)
        st_ref[c] = st.astype(BF16)
        return st * dec_ref[c][0:1, :] + _dot_tn(v_ref[rows, :], ks_ref[SCAN_LEVELS, rows, :])

    def output(c):
        rows = chunk_rows(c)
        o = _dot_nt(qs_ref[1 + SCAN_LEVELS, rows, :], st_ref[c])
        o = o + _dot(at_ref[rows, :], v_ref[rows, :])
        o_ref[rows, :] = (_rms(o) * gn * g_ref[rows, :].astype(F32)).astype(BF16)

    def stage(group, st, do_output, do_scores, do_scale):
        for i in range(SCAN_GROUP):
            if do_output:
                output((group - 2) * SCAN_GROUP + i)
        for i in range(SCAN_GROUP):
            if do_scores:
                st = scores((group - 1) * SCAN_GROUP + i, st)
        for i in range(SCAN_GROUP):
            if do_scale:
                scale(group * SCAN_GROUP + i)
        return st

    st = jnp.zeros((HEAD_DIM, HEAD_DIM), F32)
    st = stage(0, st, False, False, True)
    st = stage(1, st, False, True, True)
    st = lax.fori_loop(2, n_groups, lambda g, s: stage(g, s, True, True, True), st)
    st = stage(n_groups, st, True, True, False)
    stage(n_groups + 1, st, True, False, False)


def _mixers(p32, p16, gn, w_spatial, bias_full, side_weights, batch, seq):
    masks = _pair_masks()
    steps = batch * HEADS
    t = batch * seq
    blk = lambda seg: pl.BlockSpec((seq, HEAD_DIM), lambda b, h: (b, seg * HEADS + h))
    step_rows = lambda b, h: (b * HEADS + h, 0)
    gmlp_blk = lambda seg: pl.BlockSpec((t // steps, WIDTH), lambda b, h: (b * HEADS + h, seg))
    side_specs = [pl.BlockSpec((w.shape[0] // steps, w.shape[1]), step_rows)
                  for w in side_weights]
    outs = pl.pallas_call(
        functools.partial(_mixers_kernel, len(side_weights)),
        grid=(batch, HEADS),
        in_specs=[
            blk(SEG_Q), blk(SEG_F), blk(SEG_INP - N32_SEGS), blk(SEG_G - N32_SEGS),
            pl.BlockSpec(masks.shape, lambda b, h: (0, 0, 0)),
            pl.BlockSpec((1, HEAD_DIM), lambda b, h: (0, 0)),
            gmlp_blk(SEG_U - N32_SEGS), gmlp_blk(SEG_V - N32_SEGS),
            pl.BlockSpec(w_spatial.shape, lambda b, h: (0, 0, 0)),
            pl.BlockSpec(bias_full.shape, lambda b, h: (0, 0)),
        ] + side_specs,
        out_specs=[pl.BlockSpec((seq, HEAD_DIM), lambda b, h: (b, h)),
                   pl.BlockSpec((t // steps, WIDTH), step_rows)] + side_specs,
        out_shape=[jax.ShapeDtypeStruct((t, WIDTH), BF16), jax.ShapeDtypeStruct((t, WIDTH), BF16)]
        + [jax.ShapeDtypeStruct(w.shape, BF16) for w in side_weights],
        scratch_shapes=[
            pltpu.VMEM((SCAN_LEVELS + 2, seq, HEAD_DIM), BF16),
            pltpu.VMEM((SCAN_LEVELS + 1, seq, HEAD_DIM), BF16),
            pltpu.VMEM((seq // CHUNK, 8, HEAD_DIM), F32),
            pltpu.VMEM((seq, CHUNK), BF16),
            pltpu.VMEM((seq // CHUNK, HEAD_DIM, HEAD_DIM), BF16),
        ],
        compiler_params=_params("parallel", "arbitrary"),
        name="mixers",
    )(p32, p32, p16, p16, jnp.asarray(masks), gn, p16, p16, w_spatial, bias_full, *side_weights)
    return outs[0], outs[1], outs[2:]


def _gmlp_rows(u_ref, v_ref, w_ref, bias_ref, o_ref):
    t = lax.broadcasted_iota(jnp.int32, (GMLP_CHUNK, GMLP_CHUNK), 0)
    s = lax.broadcasted_iota(jnp.int32, (GMLP_CHUNK, GMLP_CHUNK), 1)
    keep = (s // CHUNK) <= (t // CHUNK)
    for h in range(HEADS):
        cols = slice(h * HEAD_DIM, (h + 1) * HEAD_DIM)
        w = jnp.where(keep, w_ref[h], 0.0).astype(BF16)
        bias = bias_ref[:, cols]
        for g in range(u_ref.shape[0] // GMLP_CHUNK):
            rows = slice(g * GMLP_CHUNK, (g + 1) * GMLP_CHUNK)
            sv = _dot(w, v_ref[rows, cols]) + bias
            o_ref[rows, cols] = (u_ref[rows, cols].astype(F32) * sv).astype(BF16)


def _merge_kernel(a_ref, b_ref, ga_ref, gb_ref, x_ref, wa_ref, wb_ref, wo_ref, o_ref, m_ref):
    for n in range(D_MODEL // MXU_COLS):
        cols = slice(n * MXU_COLS, (n + 1) * MXU_COLS)
        ya = _dot(a_ref[...], wa_ref[:, cols])
        yb = _dot(b_ref[...], wb_ref[:, cols])
        m_ref[:, cols] = (ga_ref[:, cols].astype(F32) * ya
                          + gb_ref[:, cols].astype(F32) * yb).astype(BF16)
    for n in range(D_MODEL // MXU_COLS):
        cols = slice(n * MXU_COLS, (n + 1) * MXU_COLS)
        o_ref[:, cols] = x_ref[:, cols] + _dot(m_ref[...], wo_ref[:, cols])


def _merge(oa, ob, p16, x2d, wa, wb, wo, *, tm):
    t = x2d.shape[0]
    row = lambda i: (i, 0)
    fixed = lambda i: (0, 0)
    gate0 = (SEG_GATE - N32_SEGS) * WIDTH // D_MODEL
    return pl.pallas_call(
        _merge_kernel,
        grid=(t // tm,),
        in_specs=[
            pl.BlockSpec((tm, WIDTH), row),
            pl.BlockSpec((tm, WIDTH), row),
            pl.BlockSpec((tm, D_MODEL), lambda i: (i, gate0)),
            pl.BlockSpec((tm, D_MODEL), lambda i: (i, gate0 + 1)),
            pl.BlockSpec((tm, D_MODEL), row),
            pl.BlockSpec(wa.shape, fixed),
            pl.BlockSpec(wb.shape, fixed),
            pl.BlockSpec(wo.shape, fixed),
        ],
        out_specs=pl.BlockSpec((tm, D_MODEL), row),
        out_shape=jax.ShapeDtypeStruct((t, D_MODEL), F32),
        scratch_shapes=[pltpu.VMEM((tm, D_MODEL), BF16)],
        compiler_params=_params("parallel"),
        name="merge_out",
    )(oa, ob, p16, p16, x2d, wa, wb, wo)


def _ffn_kernel(x_ref, gain_ref, w1_ref, w2_ref, o_ref, h_ref):
    j = pl.program_id(1)

    @pl.when(j == 0)
    def _():
        x = x_ref[...]
        h_ref[...] = (_rms(x) * gain_ref[...]).astype(BF16)
        o_ref[...] = x

    a = jnp.maximum(_dot(h_ref[...], w1_ref[...]), 0.0)
    o_ref[...] += _dot((a * a).astype(BF16), w2_ref[...])


def _ffn(x2d, gain, w1, w2, *, tm, tf):
    t = x2d.shape[0]
    row = lambda i, j: (i, 0)
    return pl.pallas_call(
        _ffn_kernel,
        grid=(t // tm, D_FF // tf),
        in_specs=[
            pl.BlockSpec((tm, D_MODEL), row),
            pl.BlockSpec((1, D_MODEL), lambda i, j: (0, 0)),
            pl.BlockSpec((D_MODEL, tf), lambda i, j: (0, j)),
            pl.BlockSpec((tf, D_MODEL), lambda i, j: (j, 0)),
        ],
        out_specs=pl.BlockSpec((tm, D_MODEL), row),
        out_shape=jax.ShapeDtypeStruct((t, D_MODEL), F32),
        scratch_shapes=[pltpu.VMEM((tm, D_MODEL), BF16)],
        compiler_params=_params("parallel", "arbitrary", vmem_limit_bytes=FFN_VMEM_LIMIT_BYTES),
        name="ffn",
    )(x2d, gain, w1, w2)


def _ple_kernel(x_ref, p_ref, gain_ref, wg_ref, wp_ref, gfin_ref, o_ref, h_ref):
    h_ref[...] = (_rms(x_ref[...]) * gain_ref[...]).astype(BF16)
    pb = p_ref[...].astype(BF16)
    sumsq = jnp.zeros((x_ref.shape[0], 1), F32)
    for n in range(D_MODEL // MXU_COLS):
        cols = slice(n * MXU_COLS, (n + 1) * MXU_COLS)
        gate = _sigmoid(_dot(h_ref[...], wg_ref[:, cols]))
        y = x_ref[:, cols] + gate * _dot(pb, wp_ref[:, cols])
        o_ref[:, cols] = y
        sumsq = sumsq + jnp.sum(y * y, axis=-1, keepdims=True)
    scale = lax.rsqrt(sumsq * (1.0 / D_MODEL) + EPS)
    o_ref[...] = o_ref[...] * scale * gfin_ref[...]


def _ple(x2d, p2d, gain, wg, wp, gfin, *, tm):
    t = x2d.shape[0]
    row = lambda i: (i, 0)
    fixed = lambda i: (0, 0)
    return pl.pallas_call(
        _ple_kernel,
        grid=(t // tm,),
        in_specs=[
            pl.BlockSpec((tm, D_MODEL), row),
            pl.BlockSpec((tm, PLE_DIM), row),
            pl.BlockSpec((1, D_MODEL), fixed),
            pl.BlockSpec(wg.shape, fixed),
            pl.BlockSpec(wp.shape, fixed),
            pl.BlockSpec((1, D_MODEL), fixed),
        ],
        out_specs=pl.BlockSpec((tm, D_MODEL), row),
        out_shape=jax.ShapeDtypeStruct((t, D_MODEL), F32),
        scratch_shapes=[pltpu.VMEM((tm, D_MODEL), BF16)],
        compiler_params=_params("parallel"),
        name="ple_final",
    )(x2d, p2d, gain, wg, wp, gfin)


def kernel(x, p, norm_mix, w_in, lb_logits, hgrn_norm, w_a_out, gmlp_ln_g, gmlp_ln_b,
           w_spatial, b_spatial, w_b_out, w_o, norm_ffn, w_ff1, w_ff2, norm_ple,
           w_ple_gate, w_ple_proj, norm_final):
    batch, seq, _ = x.shape
    assert w_in.shape[0] == 1 and seq % GMLP_CHUNK == 0
    t = batch * seq
    x2d = x.reshape(t, D_MODEL)
    row = lambda a: a.reshape(1, -1)

    p32, p16 = _inproj(x2d, row(norm_mix[0]), w_in[0], lb_logits,
                       row(gmlp_ln_g[0]), row(gmlp_ln_b[0]), tm=1024)
    bias_full = jnp.repeat(b_spatial[0].T, HEAD_DIM, axis=1)
    o_a, o_b, (wa, wb, wo, w1, w2, wg) = _mixers(
        p32, p16, row(hgrn_norm[0]), w_spatial[0], bias_full,
        [w_a_out[0], w_b_out[0], w_o[0], w_ff1[0], w_ff2[0], w_ple_gate[0]], batch, seq)
    x2d = _merge(o_a, o_b, p16, x2d, wa, wb, wo, tm=512)
    x2d = _ffn(x2d, row(norm_ffn[0]), w1, w2, tm=1024, tf=1024)
    x2d = _ple(x2d, p[0].reshape(t, PLE_DIM), row(norm_ple[0]), wg,
               w_ple_proj[0].astype(BF16), row(norm_final), tm=512)
    return x2d.reshape(batch, seq, D_MODEL)
```

```python
import functools

import jax
import jax.numpy as jnp
import numpy as np
from jax import lax
from jax.experimental import pallas as pl
from jax.experimental.pallas import tpu as pltpu

F32 = jnp.float32
BF16 = jnp.bfloat16

D_MODEL = 2048
WIDTH = D_MODEL // 2
HEAD_DIM = 128
HEADS = WIDTH // HEAD_DIM
CHUNK = 64
GMLP_CHUNK = 128
D_FF = 4 * D_MODEL
PLE_DIM = 256
EPS = 1e-6
LOG2_E = np.float32(np.log2(np.e))
N_IN = 6 * WIDTH + 2 * D_MODEL

SCAN_LEVELS = 6
MXU_COLS = 256
SCAN_LAG = 2

INPROJ_ROWS = 1024
MERGE_ROWS = 512
FFN_ROWS, FFN_COLS = 1024, 1024
PLE_ROWS = 512
VMEM_LIMIT_BYTES = 56 * 1024 * 1024
FFN_VMEM_LIMIT_BYTES = 62 * 1024 * 1024

SEG_Q, SEG_F, SEG_INP, SEG_G, SEG_U, SEG_V, SEG_GATE = 0, 1, 2, 3, 4, 5, 6
N32_SEGS = 2


def _rms(x):
    return x * lax.rsqrt(jnp.mean(x * x, axis=-1, keepdims=True) + EPS)


def _dot(a, b):
    return jnp.dot(a, b, preferred_element_type=F32)


def _dot_nt(a, b):
    return lax.dot_general(a, b, (((1,), (1,)), ((), ())), preferred_element_type=F32)


def _dot_tn(a, b):
    return lax.dot_general(a, b, (((0,), (0,)), ((), ())), preferred_element_type=F32)


def _gelu(x):
    return x * (lax.erf(x * np.float32(1.0 / np.sqrt(2.0))) + 1.0) * 0.5


def _sigmoid(x):
    return 0.5 * jnp.tanh(0.5 * x) + 0.5


def _params(*semantics, vmem_limit_bytes=VMEM_LIMIT_BYTES):
    return pltpu.CompilerParams(
        dimension_semantics=semantics, vmem_limit_bytes=vmem_limit_bytes)


def _inproj_kernel(x_ref, gain_ref, w_ref, lbl_ref, lng_ref, lnb_ref, o32_ref, o16_ref, h_ref):
    j = pl.program_id(1)

    @pl.when(j == 0)
    def _():
        h_ref[...] = (_rms(x_ref[...]) * gain_ref[...]).astype(BF16)

    def emit(o_ref, act):
        for n in range(WIDTH // MXU_COLS):
            cols = slice(n * MXU_COLS, (n + 1) * MXU_COLS)
            o_ref[:, cols] = act(_dot(h_ref[...], w_ref[:, cols].astype(BF16)), cols).astype(o_ref.dtype)

    silu = lambda a, cols: a * _sigmoid(a)

    @pl.when(j == SEG_Q)
    def _():
        emit(o32_ref, silu)

    @pl.when(j == SEG_F)
    def _():
        logits = lbl_ref[...]
        e = jnp.exp(logits - jnp.max(logits, axis=0, keepdims=True))
        lb = (e / jnp.sum(e, axis=0, keepdims=True))[0:1, :]
        emit(o32_ref, lambda a, cols: LOG2_E * jnp.log(
            lb[:, cols] + (1.0 - lb[:, cols]) * jax.nn.sigmoid(a)))

    @pl.when(j == SEG_INP)
    def _():
        emit(o16_ref, lambda a, cols: a)

    @pl.when(j == SEG_G)
    def _():
        emit(o16_ref, silu)

    @pl.when(j == SEG_U)
    def _():
        emit(o16_ref, lambda a, cols: _gelu(a))

    @pl.when(j == SEG_V)
    def _():
        a = _gelu(_dot(h_ref[...], w_ref[...].astype(BF16)))
        mu = jnp.mean(a, axis=-1, keepdims=True)
        c = a - mu
        var = jnp.mean(c * c, axis=-1, keepdims=True)
        o16_ref[...] = (c * lax.rsqrt(var + EPS) * lng_ref[...] + lnb_ref[...]).astype(BF16)

    @pl.when(j >= SEG_GATE)
    def _():
        emit(o16_ref, lambda a, cols: _sigmoid(a))


def _inproj(x2d, gain, w, lb_logits, ln_g, ln_b, *, tm):
    t = x2d.shape[0]
    row = lambda i, j: (i, 0)
    fixed = lambda i, j: (0, 0)
    n32 = N32_SEGS * WIDTH
    return pl.pallas_call(
        _inproj_kernel,
        grid=(t // tm, N_IN // WIDTH),
        in_specs=[
            pl.BlockSpec((tm, D_MODEL), row),
            pl.BlockSpec((1, D_MODEL), fixed),
            pl.BlockSpec((D_MODEL, WIDTH), lambda i, j: (0, j)),
            pl.BlockSpec(lb_logits.shape, fixed),
            pl.BlockSpec((1, WIDTH), fixed),
            pl.BlockSpec((1, WIDTH), fixed),
        ],
        out_specs=[
            pl.BlockSpec((tm, WIDTH), lambda i, j: (i, jnp.minimum(j, N32_SEGS - 1))),
            pl.BlockSpec((tm, WIDTH), lambda i, j: (i, jnp.maximum(j - N32_SEGS, 0))),
        ],
        out_shape=[jax.ShapeDtypeStruct((t, n32), F32),
                   jax.ShapeDtypeStruct((t, N_IN - n32), BF16)],
        scratch_shapes=[pltpu.VMEM((tm, D_MODEL), BF16)],
        compiler_params=_params("parallel", "arbitrary"),
        name="inproj",
    )(x2d, gain, w, lb_logits, ln_g, ln_b)


def _pair_masks():
    t = np.arange(CHUNK)[:, None]
    s = np.arange(CHUNK)[None, :]
    masks = []
    for lvl in range(SCAN_LEVELS):
        same = (t >> (lvl + 1)) == (s >> (lvl + 1))
        masks.append(same & (((t >> lvl) & 1) == 1) & (((s >> lvl) & 1) == 0))
    masks.append(t == s)
    return np.stack(masks).astype(np.float32)


def _level_exponents(p, lvl):
    d = p.shape[-1]
    half = 1 << lvl
    if half >= 8:
        p4 = p.reshape(CHUNK // (2 * half), 2, half, d)
        lo, up = p4[:, 0], p4[:, 1]
        ref = jnp.broadcast_to(lo[:, half - 1:half, :], lo.shape)
        e = jnp.stack([ref - lo, up], axis=1).reshape(CHUNK, d)
        return e, jnp.stack([lo, up + ref], axis=1).reshape(CHUNK, d)
    sub = lax.broadcasted_iota(jnp.int32, (CHUNK, d), 0)
    upper = ((sub >> lvl) & 1) == 1
    prev1 = pltpu.roll(p, 1, axis=0)
    if lvl == 0:
        return jnp.where(upper, p, 0.0), p + jnp.where(upper, prev1, 0.0)
    if lvl == 1:
        pos = sub & 3
        nxt1 = pltpu.roll(p, CHUNK - 1, axis=0)
        prev2 = pltpu.roll(p, 2, axis=0)
        ref = jnp.where(pos == 0, nxt1, jnp.where(pos == 1, p, jnp.where(pos == 2, prev1, prev2)))
    else:
        p3 = p.reshape(CHUNK // 8, 8, d)
        ref = jnp.broadcast_to(p3[:, 3:4, :], p3.shape).reshape(CHUNK, d)
    return jnp.where(upper, p, ref - p), p + jnp.where(upper, ref, 0.0)


def _mixers_kernel(n_side, q_ref, lf_ref, v_ref, g_ref, mask_ref, gn_ref,
                   u_ref, gv_ref, ws_ref, bias_ref, *refs):
    o_ref, ob_ref = refs[n_side], refs[n_side + 1]
    gn = gn_ref[...]

    def chunk_rows(c):
        start = c * CHUNK
        return pl.ds(start if isinstance(c, int) else pl.multiple_of(start, CHUNK), CHUNK)

    def scores(c, st):
        rows = chunk_rows(c)
        lf = lf_ref[rows, :]
        qb = q_ref[rows, :].astype(BF16)
        kb = (1.0 - jnp.exp2(lf)).astype(BF16)
        s_diag = _dot_nt(qb, kb)
        s_lvl = []
        p = lf
        for lvl in range(SCAN_LEVELS):
            e, p = _level_exponents(p, lvl)
            wb = jnp.exp2(e).astype(BF16)
            s_lvl.append(_dot_nt(qb * wb, kb * wb if lvl > 0 else kb))
        last = p[CHUNK - 1:CHUNK, :]
        qe = qb * jnp.exp2(p).astype(BF16)
        k_end = kb * jnp.exp2(last - p).astype(BF16)
        tiles = []
        for r in range(CHUNK // 8):
            sl = slice(8 * r, 8 * r + 8)
            acc = mask_ref[SCAN_LEVELS, sl, :] * s_diag[sl]
            for lvl in range(SCAN_LEVELS):
                if lvl < 3 or (r >> (lvl - 3)) & 1:
                    acc = acc + mask_ref[lvl, sl, :] * s_lvl[lvl][sl]
            tiles.append(acc)
        attn = jnp.concatenate(tiles, axis=0).astype(BF16)
        pending = (qe, attn, st.astype(BF16))
        return st * jnp.exp2(last) + _dot_tn(v_ref[rows, :], k_end), pending

    def output(c, pending):
        rows = chunk_rows(c)
        qe, attn, st_in = pending
        o = _dot_nt(qe, st_in) + _dot(attn, v_ref[rows, :])
        o_ref[rows, :] = (_rms(o) * gn * g_ref[rows, :].astype(F32)).astype(BF16)

    n_chunks = q_ref.shape[0] // CHUNK
    st = jnp.zeros((HEAD_DIM, HEAD_DIM), F32)
    pending = {}
    for c in range(n_chunks + SCAN_LAG):
        if c < n_chunks:
            st, pending[c] = scores(c, st)
        if c >= SCAN_LAG:
            output(c - SCAN_LAG, pending.pop(c - SCAN_LAG))
    _gmlp_rows(u_ref, gv_ref, ws_ref, bias_ref, ob_ref)
    for src_ref, dst_ref in zip(refs[:n_side], refs[n_side + 2:2 * n_side + 2]):
        dst_ref[...] = src_ref[...].astype(BF16)


def _mixers(p32, p16, gn, w_spatial, bias_full, side_weights, batch, seq):
    masks = _pair_masks()
    steps = batch * HEADS
    t = batch * seq
    blk = lambda seg: pl.BlockSpec((seq, HEAD_DIM), lambda b, h: (b, seg * HEADS + h))
    step_rows = lambda b, h: (b * HEADS + h, 0)
    gmlp_blk = lambda seg: pl.BlockSpec((t // steps, WIDTH), lambda b, h: (b * HEADS + h, seg))
    side_specs = [pl.BlockSpec((w.shape[0] // steps, w.shape[1]), step_rows)
                  for w in side_weights]
    outs = pl.pallas_call(
        functools.partial(_mixers_kernel, len(side_weights)),
        grid=(batch, HEADS),
        in_specs=[
            blk(SEG_Q), blk(SEG_F), blk(SEG_INP - N32_SEGS), blk(SEG_G - N32_SEGS),
            pl.BlockSpec(masks.shape, lambda b, h: (0, 0, 0)),
            pl.BlockSpec((1, HEAD_DIM), lambda b, h: (0, 0)),
            gmlp_blk(SEG_U - N32_SEGS), gmlp_blk(SEG_V - N32_SEGS),
            pl.BlockSpec(w_spatial.shape, lambda b, h: (0, 0, 0)),
            pl.BlockSpec(bias_full.shape, lambda b, h: (0, 0)),
        ] + side_specs,
        out_specs=[pl.BlockSpec((seq, HEAD_DIM), lambda b, h: (b, h)),
                   pl.BlockSpec((t // steps, WIDTH), step_rows)] + side_specs,
        out_shape=[jax.ShapeDtypeStruct((t, WIDTH), BF16), jax.ShapeDtypeStruct((t, WIDTH), BF16)]
        + [jax.ShapeDtypeStruct(w.shape, BF16) for w in side_weights],
        compiler_params=_params("parallel", "arbitrary"),
        name="mixers",
    )(p32, p32, p16, p16, jnp.asarray(masks), gn, p16, p16, w_spatial, bias_full, *side_weights)
    return outs[0], outs[1], outs[2:]


def _gmlp_rows(u_ref, v_ref, w_ref, bias_ref, o_ref):
    t = lax.broadcasted_iota(jnp.int32, (GMLP_CHUNK, GMLP_CHUNK), 0)
    s = lax.broadcasted_iota(jnp.int32, (GMLP_CHUNK, GMLP_CHUNK), 1)
    keep = (s // CHUNK) <= (t // CHUNK)
    for h in range(HEADS):
        cols = slice(h * HEAD_DIM, (h + 1) * HEAD_DIM)
        w = jnp.where(keep, w_ref[h], 0.0).astype(BF16)
        bias = bias_ref[:, cols]
        for g in range(u_ref.shape[0] // GMLP_CHUNK):
            rows = slice(g * GMLP_CHUNK, (g + 1) * GMLP_CHUNK)
            sv = _dot(w, v_ref[rows, cols]) + bias
            o_ref[rows, cols] = (u_ref[rows, cols].astype(F32) * sv).astype(BF16)


def _merge_kernel(a_ref, b_ref, ga_ref, gb_ref, x_ref, wa_ref, wb_ref, wo_ref, o_ref, m_ref):
    for n in range(D_MODEL // MXU_COLS):
        cols = slice(n * MXU_COLS, (n + 1) * MXU_COLS)
        ya = _dot(a_ref[...], wa_ref[:, cols])
        yb = _dot(b_ref[...], wb_ref[:, cols])
        m_ref[:, cols] = (ga_ref[:, cols].astype(F32) * ya
                          + gb_ref[:, cols].astype(F32) * yb).astype(BF16)
    for n in range(D_MODEL // MXU_COLS):
        cols = slice(n * MXU_COLS, (n + 1) * MXU_COLS)
        o_ref[:, cols] = x_ref[:, cols] + _dot(m_ref[...], wo_ref[:, cols])


def _merge(oa, ob, p16, x2d, wa, wb, wo, *, tm):
    t = x2d.shape[0]
    row = lambda i: (i, 0)
    fixed = lambda i: (0, 0)
    gate0 = (SEG_GATE - N32_SEGS) * WIDTH // D_MODEL
    return pl.pallas_call(
        _merge_kernel,
        grid=(t // tm,),
        in_specs=[
            pl.BlockSpec((tm, WIDTH), row),
            pl.BlockSpec((tm, WIDTH), row),
            pl.BlockSpec((tm, D_MODEL), lambda i: (i, gate0)),
            pl.BlockSpec((tm, D_MODEL), lambda i: (i, gate0 + 1)),
            pl.BlockSpec((tm, D_MODEL), row),
            pl.BlockSpec(wa.shape, fixed),
            pl.BlockSpec(wb.shape, fixed),
            pl.BlockSpec(wo.shape, fixed),
        ],
        out_specs=pl.BlockSpec((tm, D_MODEL), row),
        out_shape=jax.ShapeDtypeStruct((t, D_MODEL), F32),
        scratch_shapes=[pltpu.VMEM((tm, D_MODEL), BF16)],
        compiler_params=_params("parallel"),
        name="merge_out",
    )(oa, ob, p16, p16, x2d, wa, wb, wo)


def _ffn_kernel(x_ref, gain_ref, w1_ref, w2_ref, o_ref, h_ref):
    j = pl.program_id(1)

    @pl.when(j == 0)
    def _():
        x = x_ref[...]
        h_ref[...] = (_rms(x) * gain_ref[...]).astype(BF16)
        o_ref[...] = x

    a = jnp.maximum(_dot(h_ref[...], w1_ref[...]), 0.0)
    o_ref[...] += _dot((a * a).astype(BF16), w2_ref[...])


def _ffn(x2d, gain, w1, w2, *, tm, tf):
    t = x2d.shape[0]
    row = lambda i, j: (i, 0)
    return pl.pallas_call(
        _ffn_kernel,
        grid=(t // tm, D_FF // tf),
        in_specs=[
            pl.BlockSpec((tm, D_MODEL), row),
            pl.BlockSpec((1, D_MODEL), lambda i, j: (0, 0)),
            pl.BlockSpec((D_MODEL, tf), lambda i, j: (0, j)),
            pl.BlockSpec((tf, D_MODEL), lambda i, j: (j, 0)),
        ],
        out_specs=pl.BlockSpec((tm, D_MODEL), row),
        out_shape=jax.ShapeDtypeStruct((t, D_MODEL), F32),
        scratch_shapes=[pltpu.VMEM((tm, D_MODEL), BF16)],
        compiler_params=_params("parallel", "arbitrary", vmem_limit_bytes=FFN_VMEM_LIMIT_BYTES),
        name="ffn",
    )(x2d, gain, w1, w2)


def _ple_kernel(x_ref, p_ref, gain_ref, wg_ref, wp_ref, gfin_ref, o_ref, h_ref):
    h_ref[...] = (_rms(x_ref[...]) * gain_ref[...]).astype(BF16)
    pb = p_ref[...].astype(BF16)
    sumsq = jnp.zeros((x_ref.shape[0], 1), F32)
    for n in range(D_MODEL // MXU_COLS):
        cols = slice(n * MXU_COLS, (n + 1) * MXU_COLS)
        gate = _sigmoid(_dot(h_ref[...], wg_ref[:, cols]))
        y = x_ref[:, cols] + gate * _dot(pb, wp_ref[:, cols])
        o_ref[:, cols] = y
        sumsq = sumsq + jnp.sum(y * y, axis=-1, keepdims=True)
    scale = lax.rsqrt(sumsq * (1.0 / D_MODEL) + EPS)
    o_ref[...] = o_ref[...] * scale * gfin_ref[...]


def _ple(x2d, p2d, gain, wg, wp, gfin, *, tm):
    t = x2d.shape[0]
    row = lambda i: (i, 0)
    fixed = lambda i: (0, 0)
    return pl.pallas_call(
        _ple_kernel,
        grid=(t // tm,),
        in_specs=[
            pl.BlockSpec((tm, D_MODEL), row),
            pl.BlockSpec((tm, PLE_DIM), row),
            pl.BlockSpec((1, D_MODEL), fixed),
            pl.BlockSpec(wg.shape, fixed),
            pl.BlockSpec(wp.shape, fixed),
            pl.BlockSpec((1, D_MODEL), fixed),
        ],
        out_specs=pl.BlockSpec((tm, D_MODEL), row),
        out_shape=jax.ShapeDtypeStruct((t, D_MODEL), F32),
        scratch_shapes=[pltpu.VMEM((tm, D_MODEL), BF16)],
        compiler_params=_params("parallel"),
        name="ple_final",
    )(x2d, p2d, gain, wg, wp, gfin)


def kernel(x, p, norm_mix, w_in, lb_logits, hgrn_norm, w_a_out, gmlp_ln_g, gmlp_ln_b,
           w_spatial, b_spatial, w_b_out, w_o, norm_ffn, w_ff1, w_ff2, norm_ple,
           w_ple_gate, w_ple_proj, norm_final):
    batch, seq, _ = x.shape
    assert w_in.shape[0] == 1 and seq % GMLP_CHUNK == 0
    t = batch * seq
    x2d = x.reshape(t, D_MODEL)
    row = lambda a: a.reshape(1, -1)

    p32, p16 = _inproj(x2d, row(norm_mix[0]), w_in[0], lb_logits,
                       row(gmlp_ln_g[0]), row(gmlp_ln_b[0]), tm=INPROJ_ROWS)
    bias_full = jnp.repeat(b_spatial[0].T, HEAD_DIM, axis=1)
    o_a, o_b, (wa, wb, wo, w1, w2, wg) = _mixers(
        p32, p16, row(hgrn_norm[0]), w_spatial[0], bias_full,
        [w_a_out[0], w_b_out[0], w_o[0], w_ff1[0], w_ff2[0], w_ple_gate[0]], batch, seq)
    x2d = _merge(o_a, o_b, p16, x2d, wa, wb, wo, tm=MERGE_ROWS)
    x2d = _ffn(x2d, row(norm_ffn[0]), w1, w2, tm=FFN_ROWS, tf=FFN_COLS)
    x2d = _ple(x2d, p[0].reshape(t, PLE_DIM), row(norm_ple[0]), wg,
               w_ple_proj[0].astype(BF16), row(norm_final), tm=PLE_ROWS)
    return x2d.reshape(batch, seq, D_MODEL)
```

```python
import functools

import jax
import jax.numpy as jnp
import numpy as np
from jax import lax
from jax.experimental import pallas as pl
from jax.experimental.pallas import tpu as pltpu

F32 = jnp.float32
BF16 = jnp.bfloat16

D_MODEL = 2048
WIDTH = D_MODEL // 2
HEAD_DIM = 128
HEADS = WIDTH // HEAD_DIM
CHUNK = 64
GMLP_CHUNK = 128
D_FF = 4 * D_MODEL
PLE_DIM = 256
EPS = 1e-6
LOG2_E = np.float32(np.log2(np.e))
N_IN = 6 * WIDTH + 2 * D_MODEL

SCAN_LEVELS = 6
MXU_COLS = 256
SCAN_LAG = 2

INPROJ_ROWS = 1024
MERGE_ROWS = 512
FFN_ROWS, FFN_COLS = 1024, 1024
PLE_ROWS = 512
VMEM_LIMIT_BYTES = 56 * 1024 * 1024
FFN_VMEM_LIMIT_BYTES = 62 * 1024 * 1024

SEG_Q, SEG_F, SEG_INP, SEG_G, SEG_U, SEG_V, SEG_GATE = 0, 1, 2, 3, 4, 5, 6
N32_SEGS = 2


def _rms(x):
    return x * lax.rsqrt(jnp.mean(x * x, axis=-1, keepdims=True) + EPS)


def _dot(a, b):
    return jnp.dot(a, b, preferred_element_type=F32)


def _dot_nt(a, b):
    return lax.dot_general(a, b, (((1,), (1,)), ((), ())), preferred_element_type=F32)


def _dot_tn(a, b):
    return lax.dot_general(a, b, (((0,), (0,)), ((), ())), preferred_element_type=F32)


def _gelu(x):
    return x * (lax.erf(x * np.float32(1.0 / np.sqrt(2.0))) + 1.0) * 0.5


def _sigmoid(x):
    return 0.5 * jnp.tanh(0.5 * x) + 0.5


def _params(*semantics, vmem_limit_bytes=VMEM_LIMIT_BYTES):
    return pltpu.CompilerParams(
        dimension_semantics=semantics, vmem_limit_bytes=vmem_limit_bytes)


def _inproj_kernel(x_ref, gain_ref, w_ref, lbl_ref, lng_ref, lnb_ref, o32_ref, o16_ref, h_ref):
    j = pl.program_id(1)

    @pl.when(j == 0)
    def _():
        h_ref[...] = (_rms(x_ref[...]) * gain_ref[...]).astype(BF16)

    def emit(o_ref, act):
        for n in range(WIDTH // MXU_COLS):
            cols = slice(n * MXU_COLS, (n + 1) * MXU_COLS)
            o_ref[:, cols] = act(_dot(h_ref[...], w_ref[:, cols].astype(BF16)), cols).astype(o_ref.dtype)

    silu = lambda a, cols: a * _sigmoid(a)

    @pl.when(j == SEG_Q)
    def _():
        emit(o32_ref, silu)

    @pl.when(j == SEG_F)
    def _():
        logits = lbl_ref[...]
        e = jnp.exp(logits - jnp.max(logits, axis=0, keepdims=True))
        lb = (e / jnp.sum(e, axis=0, keepdims=True))[0:1, :]
        emit(o32_ref, lambda a, cols: LOG2_E * jnp.log(
            lb[:, cols] + (1.0 - lb[:, cols]) * jax.nn.sigmoid(a)))

    @pl.when(j == SEG_INP)
    def _():
        emit(o16_ref, lambda a, cols: a)

    @pl.when(j == SEG_G)
    def _():
        emit(o16_ref, silu)

    @pl.when(j == SEG_U)
    def _():
        emit(o16_ref, lambda a, cols: _gelu(a))

    @pl.when(j == SEG_V)
    def _():
        a = _gelu(_dot(h_ref[...], w_ref[...].astype(BF16)))
        mu = jnp.mean(a, axis=-1, keepdims=True)
        c = a - mu
        var = jnp.mean(c * c, axis=-1, keepdims=True)
        o16_ref[...] = (c * lax.rsqrt(var + EPS) * lng_ref[...] + lnb_ref[...]).astype(BF16)

    @pl.when(j >= SEG_GATE)
    def _():
        emit(o16_ref, lambda a, cols: _sigmoid(a))


def _inproj(x2d, gain, w, lb_logits, ln_g, ln_b, *, tm):
    t = x2d.shape[0]
    row = lambda i, j: (i, 0)
    fixed = lambda i, j: (0, 0)
    n32 = N32_SEGS * WIDTH
    return pl.pallas_call(
        _inproj_kernel,
        grid=(t // tm, N_IN // WIDTH),
        in_specs=[
            pl.BlockSpec((tm, D_MODEL), row),
            pl.BlockSpec((1, D_MODEL), fixed),
            pl.BlockSpec((D_MODEL, WIDTH), lambda i, j: (0, j)),
            pl.BlockSpec(lb_logits.shape, fixed),
            pl.BlockSpec((1, WIDTH), fixed),
            pl.BlockSpec((1, WIDTH), fixed),
        ],
        out_specs=[
            pl.BlockSpec((tm, WIDTH), lambda i, j: (i, jnp.minimum(j, N32_SEGS - 1))),
            pl.BlockSpec((tm, WIDTH), lambda i, j: (i, jnp.maximum(j - N32_SEGS, 0))),
        ],
        out_shape=[jax.ShapeDtypeStruct((t, n32), F32),
                   jax.ShapeDtypeStruct((t, N_IN - n32), BF16)],
        scratch_shapes=[pltpu.VMEM((tm, D_MODEL), BF16)],
        compiler_params=_params("parallel", "arbitrary"),
        name="inproj",
    )(x2d, gain, w, lb_logits, ln_g, ln_b)


def _pair_masks():
    t = np.arange(CHUNK)[:, None]
    s = np.arange(CHUNK)[None, :]
    masks = []
    for lvl in range(SCAN_LEVELS):
        same = (t >> (lvl + 1)) == (s >> (lvl + 1))
        masks.append(same & (((t >> lvl) & 1) == 1) & (((s >> lvl) & 1) == 0))
    masks.append(t == s)
    return np.stack(masks).astype(np.float32)


def _row_masks(d):
    sub = lax.broadcasted_iota(jnp.int32, (1, 8, d), 1)
    pos = sub & 3
    return {"upper": [((sub >> lvl) & 1) == 1 for lvl in range(3)],
            "pos": [pos == k for k in range(3)]}


def _level_exponents(p, lvl, masks):
    d = p.shape[-1]
    half = 1 << lvl
    if half >= 8:
        p4 = p.reshape(CHUNK // (2 * half), 2, half, d)
        lo, up = p4[:, 0], p4[:, 1]
        ref = jnp.broadcast_to(lo[:, half - 1:half, :], lo.shape)
        e = jnp.stack([ref - lo, up], axis=1).reshape(CHUNK, d)
        return e, jnp.stack([lo, up + ref], axis=1).reshape(CHUNK, d)
    p3 = p.reshape(CHUNK // 8, 8, d)
    roll = lambda shift: pltpu.roll(p3, shift, axis=1)
    upper = masks["upper"][lvl]
    if lvl == 0:
        e, nxt = jnp.where(upper, p3, 0.0), p3 + jnp.where(upper, roll(1), 0.0)
    else:
        if lvl == 1:
            is0, is1, is2 = masks["pos"]
            ref = jnp.where(is0, roll(7), jnp.where(is1, p3, jnp.where(is2, roll(1), roll(2))))
        else:
            ref = jnp.broadcast_to(p3[:, 3:4, :], p3.shape)
        e, nxt = jnp.where(upper, p3, ref - p3), p3 + jnp.where(upper, ref, 0.0)
    return e.reshape(CHUNK, d), nxt.reshape(CHUNK, d)


def _mixers_kernel(n_side, q_ref, lf_ref, v_ref, g_ref, mask_ref, gn_ref,
                   u_ref, gv_ref, ws_ref, bias_ref, *refs):
    o_ref, ob_ref = refs[n_side], refs[n_side + 1]
    gn = gn_ref[...]
    masks = _row_masks(HEAD_DIM)

    def chunk_rows(c):
        start = c * CHUNK
        return pl.ds(start if isinstance(c, int) else pl.multiple_of(start, CHUNK), CHUNK)

    def scores(c, st):
        rows = chunk_rows(c)
        lf = lf_ref[rows, :]
        qb = q_ref[rows, :].astype(BF16)
        kb = (1.0 - jnp.exp2(lf)).astype(BF16)
        s_diag = _dot_nt(qb, kb)
        s_lvl = []
        p = lf
        for lvl in range(SCAN_LEVELS):
            e, p = _level_exponents(p, lvl, masks)
            wb = jnp.exp2(e).astype(BF16)
            s_lvl.append(_dot_nt(qb * wb, kb * wb if lvl > 0 else kb))
        last = p[CHUNK - 1:CHUNK, :]
        qe = qb * jnp.exp2(p).astype(BF16)
        k_end = kb * jnp.exp2(last - p).astype(BF16)
        tiles = []
        for r in range(CHUNK // 8):
            sl = slice(8 * r, 8 * r + 8)
            acc = mask_ref[SCAN_LEVELS, sl, :] * s_diag[sl]
            for lvl in range(SCAN_LEVELS):
                if lvl < 3 or (r >> (lvl - 3)) & 1:
                    acc = acc + mask_ref[lvl, sl, :] * s_lvl[lvl][sl]
            tiles.append(acc)
        attn = jnp.concatenate(tiles, axis=0).astype(BF16)
        pending = (qe, attn, st.astype(BF16))
        return st * jnp.exp2(last) + _dot_tn(v_ref[rows, :], k_end), pending

    def output(c, pending):
        rows = chunk_rows(c)
        qe, attn, st_in = pending
        o = _dot_nt(qe, st_in) + _dot(attn, v_ref[rows, :])
        o_ref[rows, :] = (_rms(o) * gn * g_ref[rows, :].astype(F32)).astype(BF16)

    n_chunks = q_ref.shape[0] // CHUNK
    st = jnp.zeros((HEAD_DIM, HEAD_DIM), F32)
    pending = {}
    for c in range(n_chunks + SCAN_LAG):
        if c < n_chunks:
            st, pending[c] = scores(c, st)
        if c >= SCAN_LAG:
            output(c - SCAN_LAG, pending.pop(c - SCAN_LAG))
    _gmlp_rows(u_ref, gv_ref, ws_ref, bias_ref, ob_ref)
    for src_ref, dst_ref in zip(refs[:n_side], refs[n_side + 2:2 * n_side + 2]):
        dst_ref[...] = src_ref[...].astype(BF16)


def _mixers(p32, p16, gn, w_spatial, bias_full, side_weights, batch, seq):
    masks = _pair_masks()
    steps = batch * HEADS
    t = batch * seq
    blk = lambda seg: pl.BlockSpec((seq, HEAD_DIM), lambda b, h: (b, seg * HEADS + h))
    step_rows = lambda b, h: (b * HEADS + h, 0)
    gmlp_blk = lambda seg: pl.BlockSpec((t // steps, WIDTH), lambda b, h: (b * HEADS + h, seg))
    side_specs = [pl.BlockSpec((w.shape[0] // steps, w.shape[1]), step_rows)
                  for w in side_weights]
    outs = pl.pallas_call(
        functools.partial(_mixers_kernel, len(side_weights)),
        grid=(batch, HEADS),
        in_specs=[
            blk(SEG_Q), blk(SEG_F), blk(SEG_INP - N32_SEGS), blk(SEG_G - N32_SEGS),
            pl.BlockSpec(masks.shape, lambda b, h: (0, 0, 0)),
            pl.BlockSpec((1, HEAD_DIM), lambda b, h: (0, 0)),
            gmlp_blk(SEG_U - N32_SEGS), gmlp_blk(SEG_V - N32_SEGS),
            pl.BlockSpec(w_spatial.shape, lambda b, h: (0, 0, 0)),
            pl.BlockSpec(bias_full.shape, lambda b, h: (0, 0)),
        ] + side_specs,
        out_specs=[pl.BlockSpec((seq, HEAD_DIM), lambda b, h: (b, h)),
                   pl.BlockSpec((t // steps, WIDTH), step_rows)] + side_specs,
        out_shape=[jax.ShapeDtypeStruct((t, WIDTH), BF16), jax.ShapeDtypeStruct((t, WIDTH), BF16)]
        + [jax.ShapeDtypeStruct(w.shape, BF16) for w in side_weights],
        compiler_params=_params("parallel", "arbitrary"),
        name="mixers",
    )(p32, p32, p16, p16, jnp.asarray(masks), gn, p16, p16, w_spatial, bias_full, *side_weights)
    return outs[0], outs[1], outs[2:]


def _gmlp_rows(u_ref, v_ref, w_ref, bias_ref, o_ref):
    t = lax.broadcasted_iota(jnp.int32, (GMLP_CHUNK, GMLP_CHUNK), 0)
    s = lax.broadcasted_iota(jnp.int32, (GMLP_CHUNK, GMLP_CHUNK), 1)
    keep = (s // CHUNK) <= (t // CHUNK)
    for h in range(HEADS):
        cols = slice(h * HEAD_DIM, (h + 1) * HEAD_DIM)
        w = jnp.where(keep, w_ref[h], 0.0).astype(BF16)
        bias = bias_ref[:, cols]
        for g in range(u_ref.shape[0] // GMLP_CHUNK):
            rows = slice(g * GMLP_CHUNK, (g + 1) * GMLP_CHUNK)
            sv = _dot(w, v_ref[rows, cols]) + bias
            o_ref[rows, cols] = (u_ref[rows, cols].astype(F32) * sv).astype(BF16)


def _merge_kernel(a_ref, b_ref, ga_ref, gb_ref, x_ref, wa_ref, wb_ref, wo_ref, o_ref, m_ref):
    for n in range(D_MODEL // MXU_COLS):
        cols = slice(n * MXU_COLS, (n + 1) * MXU_COLS)
        ya = _dot(a_ref[...], wa_ref[:, cols])
        yb = _dot(b_ref[...], wb_ref[:, cols])
        m_ref[:, cols] = (ga_ref[:, cols].astype(F32) * ya
                          + gb_ref[:, cols].astype(F32) * yb).astype(BF16)
    for n in range(D_MODEL // MXU_COLS):
        cols = slice(n * MXU_COLS, (n + 1) * MXU_COLS)
        o_ref[:, cols] = x_ref[:, cols] + _dot(m_ref[...], wo_ref[:, cols])


def _merge(oa, ob, p16, x2d, wa, wb, wo, *, tm):
    t = x2d.shape[0]
    row = lambda i: (i, 0)
    fixed = lambda i: (0, 0)
    gate0 = (SEG_GATE - N32_SEGS) * WIDTH // D_MODEL
    return pl.pallas_call(
        _merge_kernel,
        grid=(t // tm,),
        in_specs=[
            pl.BlockSpec((tm, WIDTH), row),
            pl.BlockSpec((tm, WIDTH), row),
            pl.BlockSpec((tm, D_MODEL), lambda i: (i, gate0)),
            pl.BlockSpec((tm, D_MODEL), lambda i: (i, gate0 + 1)),
            pl.BlockSpec((tm, D_MODEL), row),
            pl.BlockSpec(wa.shape, fixed),
            pl.BlockSpec(wb.shape, fixed),
            pl.BlockSpec(wo.shape, fixed),
        ],
        out_specs=pl.BlockSpec((tm, D_MODEL), row),
        out_shape=jax.ShapeDtypeStruct((t, D_MODEL), F32),
        scratch_shapes=[pltpu.VMEM((tm, D_MODEL), BF16)],
        compiler_params=_params("parallel"),
        name="merge_out",
    )(oa, ob, p16, p16, x2d, wa, wb, wo)


def _ffn_kernel(x_ref, gain_ref, w1_ref, w2_ref, o_ref, h_ref):
    j = pl.program_id(1)

    @pl.when(j == 0)
    def _():
        x = x_ref[...]
        h_ref[...] = (_rms(x) * gain_ref[...]).astype(BF16)
        o_ref[...] = x

    a = jnp.maximum(_dot(h_ref[...], w1_ref[...]), 0.0)
    o_ref[...] += _dot((a * a).astype(BF16), w2_ref[...])


def _ffn(x2d, gain, w1, w2, *, tm, tf):
    t = x2d.shape[0]
    row = lambda i, j: (i, 0)
    return pl.pallas_call(
        _ffn_kernel,
        grid=(t // tm, D_FF // tf),
        in_specs=[
            pl.BlockSpec((tm, D_MODEL), row),
            pl.BlockSpec((1, D_MODEL), lambda i, j: (0, 0)),
            pl.BlockSpec((D_MODEL, tf), lambda i, j: (0, j)),
            pl.BlockSpec((tf, D_MODEL), lambda i, j: (j, 0)),
        ],
        out_specs=pl.BlockSpec((tm, D_MODEL), row),
        out_shape=jax.ShapeDtypeStruct((t, D_MODEL), F32),
        scratch_shapes=[pltpu.VMEM((tm, D_MODEL), BF16)],
        compiler_params=_params("parallel", "arbitrary", vmem_limit_bytes=FFN_VMEM_LIMIT_BYTES),
        name="ffn",
    )(x2d, gain, w1, w2)


def _ple_kernel(x_ref, p_ref, gain_ref, wg_ref, wp_ref, gfin_ref, o_ref, h_ref):
    h_ref[...] = (_rms(x_ref[...]) * gain_ref[...]).astype(BF16)
    pb = p_ref[...].astype(BF16)
    sumsq = jnp.zeros((x_ref.shape[0], 1), F32)
    for n in range(D_MODEL // MXU_COLS):
        cols = slice(n * MXU_COLS, (n + 1) * MXU_COLS)
        gate = _sigmoid(_dot(h_ref[...], wg_ref[:, cols]))
        y = x_ref[:, cols] + gate * _dot(pb, wp_ref[:, cols])
        o_ref[:, cols] = y
        sumsq = sumsq + jnp.sum(y * y, axis=-1, keepdims=True)
    scale = lax.rsqrt(sumsq * (1.0 / D_MODEL) + EPS)
    o_ref[...] = o_ref[...] * scale * gfin_ref[...]


def _ple(x2d, p2d, gain, wg, wp, gfin, *, tm):
    t = x2d.shape[0]
    row = lambda i: (i, 0)
    fixed = lambda i: (0, 0)
    return pl.pallas_call(
        _ple_kernel,
        grid=(t // tm,),
        in_specs=[
            pl.BlockSpec((tm, D_MODEL), row),
            pl.BlockSpec((tm, PLE_DIM), row),
            pl.BlockSpec((1, D_MODEL), fixed),
            pl.BlockSpec(wg.shape, fixed),
            pl.BlockSpec(wp.shape, fixed),
            pl.BlockSpec((1, D_MODEL), fixed),
        ],
        out_specs=pl.BlockSpec((tm, D_MODEL), row),
        out_shape=jax.ShapeDtypeStruct((t, D_MODEL), F32),
        scratch_shapes=[pltpu.VMEM((tm, D_MODEL), BF16)],
        compiler_params=_params("parallel"),
        name="ple_final",
    )(x2d, p2d, gain, wg, wp, gfin)


def kernel(x, p, norm_mix, w_in, lb_logits, hgrn_norm, w_a_out, gmlp_ln_g, gmlp_ln_b,
           w_spatial, b_spatial, w_b_out, w_o, norm_ffn, w_ff1, w_ff2, norm_ple,
           w_ple_gate, w_ple_proj, norm_final):
    batch, seq, _ = x.shape
    assert w_in.shape[0] == 1 and seq % GMLP_CHUNK == 0
    t = batch * seq
    x2d = x.reshape(t, D_MODEL)
    row = lambda a: a.reshape(1, -1)

    p32, p16 = _inproj(x2d, row(norm_mix[0]), w_in[0], lb_logits,
                       row(gmlp_ln_g[0]), row(gmlp_ln_b[0]), tm=INPROJ_ROWS)
    bias_full = jnp.repeat(b_spatial[0].T, HEAD_DIM, axis=1)
    o_a, o_b, (wa, wb, wo, w1, w2, wg) = _mixers(
        p32, p16, row(hgrn_norm[0]), w_spatial[0], bias_full,
        [w_a_out[0], w_b_out[0], w_o[0], w_ff1[0], w_ff2[0], w_ple_gate[0]], batch, seq)
    x2d = _merge(o_a, o_b, p16, x2d, wa, wb, wo, tm=MERGE_ROWS)
    x2d = _ffn(x2d, row(norm_ffn[0]), w1, w2, tm=FFN_ROWS, tf=FFN_COLS)
    x2d = _ple(x2d, p[0].reshape(t, PLE_DIM), row(norm_ple[0]), wg,
               w_ple_proj[0].astype(BF16), row(norm_final), tm=PLE_ROWS)
    return x2d.reshape(batch, seq, D_MODEL)
```

```python
import functools

import jax
import jax.numpy as jnp
import numpy as np
from jax import lax
from jax.experimental import pallas as pl
from jax.experimental.pallas import tpu as pltpu

F32 = jnp.float32
BF16 = jnp.bfloat16

D_MODEL = 2048
WIDTH = D_MODEL // 2
HEAD_DIM = 128
HEADS = WIDTH // HEAD_DIM
CHUNK = 64
GMLP_CHUNK = 128
D_FF = 4 * D_MODEL
PLE_DIM = 256
EPS = 1e-6
LOG2_E = np.float32(np.log2(np.e))
N_IN = 6 * WIDTH + 2 * D_MODEL

SCAN_LEVELS = 6
MXU_COLS = 256
SCAN_LAG = 2

INPROJ_ROWS = 1024
MERGE_ROWS = 512
FFN_ROWS, FFN_COLS = 1024, 1024
PLE_ROWS = 512
VMEM_LIMIT_BYTES = 56 * 1024 * 1024
BIG_VMEM_LIMIT_BYTES = 62 * 1024 * 1024

SEG_Q, SEG_F, SEG_INP, SEG_G, SEG_U, SEG_V, SEG_GATE = 0, 1, 2, 3, 4, 5, 6
N32_SEGS = 2


def _rms(x):
    return x * lax.rsqrt(jnp.mean(x * x, axis=-1, keepdims=True) + EPS)


def _dot(a, b):
    return jnp.dot(a, b, preferred_element_type=F32)


def _dot_nt(a, b):
    return lax.dot_general(a, b, (((1,), (1,)), ((), ())), preferred_element_type=F32)


def _dot_tn(a, b):
    return lax.dot_general(a, b, (((0,), (0,)), ((), ())), preferred_element_type=F32)


def _gelu(x):
    return x * (lax.erf(x * np.float32(1.0 / np.sqrt(2.0))) + 1.0) * 0.5


def _sigmoid(x):
    return 0.5 * jnp.tanh(0.5 * x) + 0.5


def _params(*semantics, vmem_limit_bytes=VMEM_LIMIT_BYTES):
    return pltpu.CompilerParams(
        dimension_semantics=semantics, vmem_limit_bytes=vmem_limit_bytes)


def _inproj_kernel(x_ref, gain_ref, w_ref, lbl_ref, lng_ref, lnb_ref, o32_ref, o16_ref, h_ref):
    j = pl.program_id(1)

    @pl.when(j == 0)
    def _():
        h_ref[...] = (_rms(x_ref[...]) * gain_ref[...]).astype(BF16)

    def emit(o_ref, act):
        for n in range(WIDTH // MXU_COLS):
            cols = slice(n * MXU_COLS, (n + 1) * MXU_COLS)
            o_ref[:, cols] = act(_dot(h_ref[...], w_ref[:, cols].astype(BF16)), cols).astype(o_ref.dtype)

    silu = lambda a, cols: a * _sigmoid(a)

    @pl.when(j == SEG_Q)
    def _():
        emit(o32_ref, silu)

    @pl.when(j == SEG_F)
    def _():
        logits = lbl_ref[...]
        e = jnp.exp(logits - jnp.max(logits, axis=0, keepdims=True))
        lb = (e / jnp.sum(e, axis=0, keepdims=True))[0:1, :]
        emit(o32_ref, lambda a, cols: LOG2_E * jnp.log(
            lb[:, cols] + (1.0 - lb[:, cols]) * jax.nn.sigmoid(a)))

    @pl.when(j == SEG_INP)
    def _():
        emit(o16_ref, lambda a, cols: a)

    @pl.when(j == SEG_G)
    def _():
        emit(o16_ref, silu)

    @pl.when(j == SEG_U)
    def _():
        emit(o16_ref, lambda a, cols: _gelu(a))

    @pl.when(j == SEG_V)
    def _():
        a = _gelu(_dot(h_ref[...], w_ref[...].astype(BF16)))
        mu = jnp.mean(a, axis=-1, keepdims=True)
        c = a - mu
        var = jnp.mean(c * c, axis=-1, keepdims=True)
        o16_ref[...] = (c * lax.rsqrt(var + EPS) * lng_ref[...] + lnb_ref[...]).astype(BF16)

    @pl.when(j >= SEG_GATE)
    def _():
        emit(o16_ref, lambda a, cols: _sigmoid(a))


def _inproj(x2d, gain, w, lb_logits, ln_g, ln_b, *, tm):
    t = x2d.shape[0]
    row = lambda i, j: (i, 0)
    fixed = lambda i, j: (0, 0)
    n32 = N32_SEGS * WIDTH
    return pl.pallas_call(
        _inproj_kernel,
        grid=(t // tm, N_IN // WIDTH),
        in_specs=[
            pl.BlockSpec((tm, D_MODEL), row),
            pl.BlockSpec((1, D_MODEL), fixed),
            pl.BlockSpec((D_MODEL, WIDTH), lambda i, j: (0, j)),
            pl.BlockSpec(lb_logits.shape, fixed),
            pl.BlockSpec((1, WIDTH), fixed),
            pl.BlockSpec((1, WIDTH), fixed),
        ],
        out_specs=[
            pl.BlockSpec((tm, WIDTH), lambda i, j: (i, jnp.minimum(j, N32_SEGS - 1))),
            pl.BlockSpec((tm, WIDTH), lambda i, j: (i, jnp.maximum(j - N32_SEGS, 0))),
        ],
        out_shape=[jax.ShapeDtypeStruct((t, n32), F32),
                   jax.ShapeDtypeStruct((t, N_IN - n32), BF16)],
        scratch_shapes=[pltpu.VMEM((tm, D_MODEL), BF16)],
        compiler_params=_params("parallel", "arbitrary"),
        name="inproj",
    )(x2d, gain, w, lb_logits, ln_g, ln_b)


def _pair_masks():
    t = np.arange(CHUNK)[:, None]
    s = np.arange(CHUNK)[None, :]
    masks = []
    for lvl in range(SCAN_LEVELS):
        same = (t >> (lvl + 1)) == (s >> (lvl + 1))
        masks.append(same & (((t >> lvl) & 1) == 1) & (((s >> lvl) & 1) == 0))
    masks.append(t == s)
    return np.stack(masks).astype(np.float32)


def _row_masks(d):
    sub = lax.broadcasted_iota(jnp.int32, (1, 8, d), 1)
    pos = sub & 3
    return {"upper": [((sub >> lvl) & 1) == 1 for lvl in range(3)],
            "pos": [pos == k for k in range(3)]}


def _level_exponents(p, lvl, masks):
    d = p.shape[-1]
    half = 1 << lvl
    if half >= 8:
        p4 = p.reshape(CHUNK // (2 * half), 2, half, d)
        lo, up = p4[:, 0], p4[:, 1]
        ref = jnp.broadcast_to(lo[:, half - 1:half, :], lo.shape)
        e = jnp.stack([ref - lo, up], axis=1).reshape(CHUNK, d)
        return e, jnp.stack([lo, up + ref], axis=1).reshape(CHUNK, d)
    p3 = p.reshape(CHUNK // 8, 8, d)
    roll = lambda shift: pltpu.roll(p3, shift, axis=1)
    upper = masks["upper"][lvl]
    if lvl == 0:
        e, nxt = jnp.where(upper, p3, 0.0), p3 + jnp.where(upper, roll(1), 0.0)
    else:
        if lvl == 1:
            is0, is1, is2 = masks["pos"]
            ref = jnp.where(is0, roll(7), jnp.where(is1, p3, jnp.where(is2, roll(1), roll(2))))
        else:
            ref = jnp.broadcast_to(p3[:, 3:4, :], p3.shape)
        e, nxt = jnp.where(upper, p3, ref - p3), p3 + jnp.where(upper, ref, 0.0)
    return e.reshape(CHUNK, d), nxt.reshape(CHUNK, d)


def _mixers_kernel(n_side, q_ref, lf_ref, v_ref, g_ref, mask_ref, gn_ref,
                   u_ref, gv_ref, ws_ref, bias_ref, *refs):
    o_ref, ob_ref = refs[n_side], refs[n_side + 1]
    gn = gn_ref[...]
    masks = _row_masks(HEAD_DIM)

    def chunk_rows(c):
        return pl.ds(c * CHUNK, CHUNK)

    def scores(c, st):
        rows = chunk_rows(c)
        lf = lf_ref[rows, :]
        qb = q_ref[rows, :].astype(BF16)
        kb = (1.0 - jnp.exp2(lf)).astype(BF16)
        s_diag = _dot_nt(qb, kb)
        s_lvl = []
        p = lf
        for lvl in range(SCAN_LEVELS):
            e, p = _level_exponents(p, lvl, masks)
            wb = jnp.exp2(e).astype(BF16)
            s_lvl.append(_dot_nt(qb * wb, kb * wb if lvl > 0 else kb))
        last = p[CHUNK - 1:CHUNK, :]
        qe = qb * jnp.exp2(p).astype(BF16)
        k_end = kb * jnp.exp2(last - p).astype(BF16)
        tiles = []
        for r in range(CHUNK // 8):
            sl = slice(8 * r, 8 * r + 8)
            acc = mask_ref[SCAN_LEVELS, sl, :] * s_diag[sl]
            for lvl in range(SCAN_LEVELS):
                if lvl < 3 or (r >> (lvl - 3)) & 1:
                    acc = acc + mask_ref[lvl, sl, :] * s_lvl[lvl][sl]
            tiles.append(acc)
        attn = jnp.concatenate(tiles, axis=0).astype(BF16)
        pending = (qe, attn, st.astype(BF16))
        return st * jnp.exp2(last) + _dot_tn(v_ref[rows, :], k_end), pending

    def output(c, pending):
        rows = chunk_rows(c)
        qe, attn, st_in = pending
        o = _dot_nt(qe, st_in) + _dot(attn, v_ref[rows, :])
        o_ref[rows, :] = (_rms(o) * gn * g_ref[rows, :].astype(F32)).astype(BF16)

    n_chunks = q_ref.shape[0] // CHUNK
    st = jnp.zeros((HEAD_DIM, HEAD_DIM), F32)
    pending = {}
    for c in range(n_chunks + SCAN_LAG):
        if c < n_chunks:
            st, pending[c] = scores(c, st)
        if c >= SCAN_LAG:
            output(c - SCAN_LAG, pending.pop(c - SCAN_LAG))
    _gmlp_rows(u_ref, gv_ref, ws_ref, bias_ref, ob_ref)
    for src_ref, dst_ref in zip(refs[:n_side], refs[n_side + 2:2 * n_side + 2]):
        dst_ref[...] = src_ref[...].astype(BF16)


def _mixers(p32, p16, gn, w_spatial, bias_full, side_weights, batch, seq):
    masks = _pair_masks()
    steps = batch * HEADS
    t = batch * seq
    blk = lambda seg: pl.BlockSpec((seq, HEAD_DIM), lambda b, h: (b, seg * HEADS + h))
    step_rows = lambda b, h: (b * HEADS + h, 0)
    gmlp_blk = lambda seg: pl.BlockSpec((t // steps, WIDTH), lambda b, h: (b * HEADS + h, seg))
    side_specs = [pl.BlockSpec((w.shape[0] // steps, w.shape[1]), step_rows)
                  for w in side_weights]
    outs = pl.pallas_call(
        functools.partial(_mixers_kernel, len(side_weights)),
        grid=(batch, HEADS),
        in_specs=[
            blk(SEG_Q), blk(SEG_F), blk(SEG_INP - N32_SEGS), blk(SEG_G - N32_SEGS),
            pl.BlockSpec(masks.shape, lambda b, h: (0, 0, 0)),
            pl.BlockSpec((1, HEAD_DIM), lambda b, h: (0, 0)),
            gmlp_blk(SEG_U - N32_SEGS), gmlp_blk(SEG_V - N32_SEGS),
            pl.BlockSpec(w_spatial.shape, lambda b, h: (0, 0, 0)),
            pl.BlockSpec(bias_full.shape, lambda b, h: (0, 0)),
        ] + side_specs,
        out_specs=[pl.BlockSpec((seq, HEAD_DIM), lambda b, h: (b, h)),
                   pl.BlockSpec((t // steps, WIDTH), step_rows)] + side_specs,
        out_shape=[jax.ShapeDtypeStruct((t, WIDTH), BF16), jax.ShapeDtypeStruct((t, WIDTH), BF16)]
        + [jax.ShapeDtypeStruct(w.shape, BF16) for w in side_weights],
        compiler_params=_params("parallel", "arbitrary"),
        name="mixers",
    )(p32, p32, p16, p16, jnp.asarray(masks), gn, p16, p16, w_spatial, bias_full, *side_weights)
    return outs[0], outs[1], outs[2:]


def _gmlp_rows(u_ref, v_ref, w_ref, bias_ref, o_ref):
    t = lax.broadcasted_iota(jnp.int32, (GMLP_CHUNK, GMLP_CHUNK), 0)
    s = lax.broadcasted_iota(jnp.int32, (GMLP_CHUNK, GMLP_CHUNK), 1)
    keep = (s // CHUNK) <= (t // CHUNK)
    for h in range(HEADS):
        cols = slice(h * HEAD_DIM, (h + 1) * HEAD_DIM)
        w = jnp.where(keep, w_ref[h], 0.0).astype(BF16)
        bias = bias_ref[:, cols]
        for g in range(u_ref.shape[0] // GMLP_CHUNK):
            rows = slice(g * GMLP_CHUNK, (g + 1) * GMLP_CHUNK)
            sv = _dot(w, v_ref[rows, cols]) + bias
            o_ref[rows, cols] = (u_ref[rows, cols].astype(F32) * sv).astype(BF16)


def _merge_kernel(n_side, a_ref, b_ref, ga_ref, gb_ref, x_ref, wa_ref, wb_ref, wo_ref, *refs):
    o_ref, m_ref = refs[n_side], refs[2 * n_side + 1]
    for n in range(D_MODEL // MXU_COLS):
        cols = slice(n * MXU_COLS, (n + 1) * MXU_COLS)
        ya = _dot(a_ref[...], wa_ref[:, cols])
        yb = _dot(b_ref[...], wb_ref[:, cols])
        m_ref[:, cols] = (ga_ref[:, cols].astype(F32) * ya
                          + gb_ref[:, cols].astype(F32) * yb).astype(BF16)
    for n in range(D_MODEL // MXU_COLS):
        cols = slice(n * MXU_COLS, (n + 1) * MXU_COLS)
        o_ref[:, cols] = x_ref[:, cols] + _dot(m_ref[...], wo_ref[:, cols])
    for src_ref, dst_ref in zip(refs[:n_side], refs[n_side + 1:2 * n_side + 1]):
        dst_ref[...] = src_ref[...].astype(BF16)


def _merge(oa, ob, p16, x2d, wa, wb, wo, side_weights, *, tm):
    t = x2d.shape[0]
    row = lambda i: (i, 0)
    fixed = lambda i: (0, 0)
    gate0 = (SEG_GATE - N32_SEGS) * WIDTH // D_MODEL
    steps = t // tm
    side_specs = [pl.BlockSpec((w.shape[0] // steps, w.shape[1]), row) for w in side_weights]
    outs = pl.pallas_call(
        functools.partial(_merge_kernel, len(side_weights)),
        grid=(steps,),
        in_specs=[
            pl.BlockSpec((tm, WIDTH), row),
            pl.BlockSpec((tm, WIDTH), row),
            pl.BlockSpec((tm, D_MODEL), lambda i: (i, gate0)),
            pl.BlockSpec((tm, D_MODEL), lambda i: (i, gate0 + 1)),
            pl.BlockSpec((tm, D_MODEL), row),
            pl.BlockSpec(wa.shape, fixed),
            pl.BlockSpec(wb.shape, fixed),
            pl.BlockSpec(wo.shape, fixed),
        ] + side_specs,
        out_specs=[pl.BlockSpec((tm, D_MODEL), row)] + side_specs,
        out_shape=[jax.ShapeDtypeStruct((t, D_MODEL), F32)]
        + [jax.ShapeDtypeStruct(w.shape, BF16) for w in side_weights],
        scratch_shapes=[pltpu.VMEM((tm, D_MODEL), BF16)],
        compiler_params=_params("parallel", vmem_limit_bytes=BIG_VMEM_LIMIT_BYTES),
        name="merge_out",
    )(oa, ob, p16, p16, x2d, wa, wb, wo, *side_weights)
    return outs[0], outs[1:]


def _ffn_kernel(x_ref, gain_ref, w1_ref, w2_ref, o_ref, h_ref):
    j = pl.program_id(1)

    @pl.when(j == 0)
    def _():
        x = x_ref[...]
        h_ref[...] = (_rms(x) * gain_ref[...]).astype(BF16)
        o_ref[...] = x

    a = jnp.maximum(_dot(h_ref[...], w1_ref[...]), 0.0)
    o_ref[...] += _dot((a * a).astype(BF16), w2_ref[...])


def _ffn(x2d, gain, w1, w2, *, tm, tf):
    t = x2d.shape[0]
    row = lambda i, j: (i, 0)
    return pl.pallas_call(
        _ffn_kernel,
        grid=(t // tm, D_FF // tf),
        in_specs=[
            pl.BlockSpec((tm, D_MODEL), row),
            pl.BlockSpec((1, D_MODEL), lambda i, j: (0, 0)),
            pl.BlockSpec((D_MODEL, tf), lambda i, j: (0, j)),
            pl.BlockSpec((tf, D_MODEL), lambda i, j: (j, 0)),
        ],
        out_specs=pl.BlockSpec((tm, D_MODEL), row),
        out_shape=jax.ShapeDtypeStruct((t, D_MODEL), F32),
        scratch_shapes=[pltpu.VMEM((tm, D_MODEL), BF16)],
        compiler_params=_params("parallel", "arbitrary", vmem_limit_bytes=BIG_VMEM_LIMIT_BYTES),
        name="ffn",
    )(x2d, gain, w1, w2)


def _ple_kernel(x_ref, p_ref, gain_ref, wg_ref, wp_ref, gfin_ref, o_ref, h_ref):
    h_ref[...] = (_rms(x_ref[...]) * gain_ref[...]).astype(BF16)
    pb = p_ref[...].astype(BF16)
    sumsq = jnp.zeros((x_ref.shape[0], 1), F32)
    for n in range(D_MODEL // MXU_COLS):
        cols = slice(n * MXU_COLS, (n + 1) * MXU_COLS)
        gate = _sigmoid(_dot(h_ref[...], wg_ref[:, cols]))
        y = x_ref[:, cols] + gate * _dot(pb, wp_ref[:, cols])
        o_ref[:, cols] = y
        sumsq = sumsq + jnp.sum(y * y, axis=-1, keepdims=True)
    scale = lax.rsqrt(sumsq * (1.0 / D_MODEL) + EPS)
    o_ref[...] = o_ref[...] * scale * gfin_ref[...]


def _ple(x2d, p2d, gain, wg, wp, gfin, *, tm):
    t = x2d.shape[0]
    row = lambda i: (i, 0)
    fixed = lambda i: (0, 0)
    return pl.pallas_call(
        _ple_kernel,
        grid=(t // tm,),
        in_specs=[
            pl.BlockSpec((tm, D_MODEL), row),
            pl.BlockSpec((tm, PLE_DIM), row),
            pl.BlockSpec((1, D_MODEL), fixed),
            pl.BlockSpec(wg.shape, fixed),
            pl.BlockSpec(wp.shape, fixed),
            pl.BlockSpec((1, D_MODEL), fixed),
        ],
        out_specs=pl.BlockSpec((tm, D_MODEL), row),
        out_shape=jax.ShapeDtypeStruct((t, D_MODEL), F32),
        scratch_shapes=[pltpu.VMEM((tm, D_MODEL), BF16)],
        compiler_params=_params("parallel"),
        name="ple_final",
    )(x2d, p2d, gain, wg, wp, gfin)


def kernel(x, p, norm_mix, w_in, lb_logits, hgrn_norm, w_a_out, gmlp_ln_g, gmlp_ln_b,
           w_spatial, b_spatial, w_b_out, w_o, norm_ffn, w_ff1, w_ff2, norm_ple,
           w_ple_gate, w_ple_proj, norm_final):
    batch, seq, _ = x.shape
    assert w_in.shape[0] == 1 and seq % GMLP_CHUNK == 0
    t = batch * seq
    x2d = x.reshape(t, D_MODEL)
    row = lambda a: a.reshape(1, -1)

    p32, p16 = _inproj(x2d, row(norm_mix[0]), w_in[0], lb_logits,
                       row(gmlp_ln_g[0]), row(gmlp_ln_b[0]), tm=INPROJ_ROWS)
    bias_full = jnp.repeat(b_spatial[0].T, HEAD_DIM, axis=1)
    o_a, o_b, (wa, wb, wo, w1, wg) = _mixers(
        p32, p16, row(hgrn_norm[0]), w_spatial[0], bias_full,
        [w_a_out[0], w_b_out[0], w_o[0], w_ff1[0], w_ple_gate[0]], batch, seq)
    x2d, (w2,) = _merge(o_a, o_b, p16, x2d, wa, wb, wo, [w_ff2[0]], tm=MERGE_ROWS)
    x2d = _ffn(x2d, row(norm_ffn[0]), w1, w2, tm=FFN_ROWS, tf=FFN_COLS)
    x2d = _ple(x2d, p[0].reshape(t, PLE_DIM), row(norm_ple[0]), wg,
               w_ple_proj[0].astype(BF16), row(norm_final), tm=PLE_ROWS)
    return x2d.reshape(batch, seq, D_MODEL)
```

```python
import functools

import jax
import jax.numpy as jnp
import numpy as np
from jax import lax
from jax.experimental import pallas as pl
from jax.experimental.pallas import tpu as pltpu

F32 = jnp.float32
BF16 = jnp.bfloat16

D_MODEL = 2048
WIDTH = D_MODEL // 2
HEAD_DIM = 128
HEADS = WIDTH // HEAD_DIM
CHUNK = 64
GMLP_CHUNK = 128
D_FF = 4 * D_MODEL
PLE_DIM = 256
EPS = 1e-6
LOG2_E = np.float32(np.log2(np.e))
N_IN = 6 * WIDTH + 2 * D_MODEL

SCAN_LEVELS = 6
MXU_COLS = 256
SCAN_LAG = 2

INPROJ_ROWS = 1024
MERGE_ROWS = 512
FFN_ROWS, FFN_COLS = 1024, 1024
PLE_ROWS = 512
VMEM_LIMIT_BYTES = 56 * 1024 * 1024
BIG_VMEM_LIMIT_BYTES = 62 * 1024 * 1024

SEG_Q, SEG_F, SEG_INP, SEG_G, SEG_U, SEG_V, SEG_GATE = 0, 1, 2, 3, 4, 5, 6
N32_SEGS = 2


def _rms(x):
    return x * lax.rsqrt(jnp.mean(x * x, axis=-1, keepdims=True) + EPS)


def _dot(a, b):
    return jnp.dot(a, b, preferred_element_type=F32)


def _dot_nt(a, b):
    return lax.dot_general(a, b, (((1,), (1,)), ((), ())), preferred_element_type=F32)


def _dot_tn(a, b):
    return lax.dot_general(a, b, (((0,), (0,)), ((), ())), preferred_element_type=F32)


def _gelu(x):
    return x * (lax.erf(x * np.float32(1.0 / np.sqrt(2.0))) + 1.0) * 0.5


def _sigmoid(x):
    return 0.5 * jnp.tanh(0.5 * x) + 0.5


def _params(*semantics, vmem_limit_bytes=VMEM_LIMIT_BYTES):
    return pltpu.CompilerParams(
        dimension_semantics=semantics, vmem_limit_bytes=vmem_limit_bytes)


def _inproj_kernel(first_rows, x_ref, gain_ref, w_ref, lbl_ref, lng_ref, lnb_ref, *refs):
    if first_rows:
        o32_ref, o16_ref, wbf_ref, h_ref = refs
    else:
        _, _, o32_ref, o16_ref, h_ref = refs
        wbf_ref = None
    j = pl.program_id(1)

    @pl.when(j == 0)
    def _():
        h_ref[...] = (_rms(x_ref[...]) * gain_ref[...]).astype(BF16)

    def w_tile(cols):
        wb = w_ref[:, cols].astype(BF16)
        if wbf_ref is not None:
            wbf_ref[:, cols] = wb
        return wb

    def emit(o_ref, act):
        for n in range(WIDTH // MXU_COLS):
            cols = slice(n * MXU_COLS, (n + 1) * MXU_COLS)
            o_ref[:, cols] = act(_dot(h_ref[...], w_tile(cols)), cols).astype(o_ref.dtype)

    silu = lambda a, cols: a * _sigmoid(a)

    @pl.when(j == SEG_Q)
    def _():
        emit(o32_ref, silu)

    @pl.when(j == SEG_F)
    def _():
        logits = lbl_ref[...]
        e = jnp.exp(logits - jnp.max(logits, axis=0, keepdims=True))
        lb = (e / jnp.sum(e, axis=0, keepdims=True))[0:1, :]
        emit(o32_ref, lambda a, cols: LOG2_E * jnp.log(
            lb[:, cols] + (1.0 - lb[:, cols]) * jax.nn.sigmoid(a)))

    @pl.when(j == SEG_INP)
    def _():
        emit(o16_ref, lambda a, cols: a)

    @pl.when(j == SEG_G)
    def _():
        emit(o16_ref, silu)

    @pl.when(j == SEG_U)
    def _():
        emit(o16_ref, lambda a, cols: _gelu(a))

    @pl.when(j == SEG_V)
    def _():
        a = _gelu(_dot(h_ref[...], w_tile(slice(None))))
        mu = jnp.mean(a, axis=-1, keepdims=True)
        c = a - mu
        var = jnp.mean(c * c, axis=-1, keepdims=True)
        o16_ref[...] = (c * lax.rsqrt(var + EPS) * lng_ref[...] + lnb_ref[...]).astype(BF16)

    @pl.when(j >= SEG_GATE)
    def _():
        emit(o16_ref, lambda a, cols: _sigmoid(a))


def _inproj(x2d, gain, w, lb_logits, ln_g, ln_b, *, tm):
    t = x2d.shape[0]
    fixed = lambda i, j: (0, 0)
    n32 = N32_SEGS * WIDTH
    n_j = N_IN // WIDTH

    def specs(first_rows):
        r = (lambda i: i) if first_rows else (lambda i: i + 1)
        ins = [
            pl.BlockSpec((tm, D_MODEL), lambda i, j: (r(i), 0)),
            pl.BlockSpec((1, D_MODEL), fixed),
            pl.BlockSpec((D_MODEL, WIDTH), lambda i, j: (0, j)),
            pl.BlockSpec(lb_logits.shape, fixed),
            pl.BlockSpec((1, WIDTH), fixed),
            pl.BlockSpec((1, WIDTH), fixed),
        ]
        outs = [
            pl.BlockSpec((tm, WIDTH), lambda i, j: (r(i), jnp.minimum(j, N32_SEGS - 1))),
            pl.BlockSpec((tm, WIDTH), lambda i, j: (r(i), jnp.maximum(j - N32_SEGS, 0))),
        ]
        return ins, outs

    out_shape = [jax.ShapeDtypeStruct((t, n32), F32), jax.ShapeDtypeStruct((t, N_IN - n32), BF16)]
    scratch = [pltpu.VMEM((tm, D_MODEL), BF16)]
    ins, outs = specs(True)
    p32, p16, w_bf = pl.pallas_call(
        functools.partial(_inproj_kernel, True),
        grid=(1, n_j),
        in_specs=ins,
        out_specs=outs + [pl.BlockSpec((D_MODEL, WIDTH), lambda i, j: (0, j))],
        out_shape=out_shape + [jax.ShapeDtypeStruct(w.shape, BF16)],
        scratch_shapes=scratch,
        compiler_params=_params("arbitrary", "arbitrary", vmem_limit_bytes=BIG_VMEM_LIMIT_BYTES),
        name="inproj_first",
    )(x2d, gain, w, lb_logits, ln_g, ln_b)
    ins, outs = specs(False)
    hbm = pl.BlockSpec(memory_space=pl.ANY)
    return pl.pallas_call(
        functools.partial(_inproj_kernel, False),
        grid=(t // tm - 1, n_j),
        in_specs=ins + [hbm, hbm],
        out_specs=outs,
        out_shape=out_shape,
        input_output_aliases={len(ins): 0, len(ins) + 1: 1},
        scratch_shapes=scratch,
        compiler_params=_params("arbitrary", "arbitrary"),
        name="inproj_rest",
    )(x2d, gain, w_bf, lb_logits, ln_g, ln_b, p32, p16)


def _pair_masks():
    t = np.arange(CHUNK)[:, None]
    s = np.arange(CHUNK)[None, :]
    masks = []
    for lvl in range(SCAN_LEVELS):
        same = (t >> (lvl + 1)) == (s >> (lvl + 1))
        masks.append(same & (((t >> lvl) & 1) == 1) & (((s >> lvl) & 1) == 0))
    masks.append(t == s)
    return np.stack(masks).astype(np.float32)


def _row_masks(d):
    sub = lax.broadcasted_iota(jnp.int32, (1, 8, d), 1)
    pos = sub & 3
    return {"upper": [((sub >> lvl) & 1) == 1 for lvl in range(3)],
            "pos": [pos == k for k in range(3)]}


def _level_exponents(p, lvl, masks):
    d = p.shape[-1]
    half = 1 << lvl
    if half >= 8:
        p4 = p.reshape(CHUNK // (2 * half), 2, half, d)
        lo, up = p4[:, 0], p4[:, 1]
        ref = jnp.broadcast_to(lo[:, half - 1:half, :], lo.shape)
        e = jnp.stack([ref - lo, up], axis=1).reshape(CHUNK, d)
        return e, jnp.stack([lo, up + ref], axis=1).reshape(CHUNK, d)
    p3 = p.reshape(CHUNK // 8, 8, d)
    roll = lambda shift: pltpu.roll(p3, shift, axis=1)
    upper = masks["upper"][lvl]
    if lvl == 0:
        e, nxt = jnp.where(upper, p3, 0.0), p3 + jnp.where(upper, roll(1), 0.0)
    else:
        if lvl == 1:
            is0, is1, is2 = masks["pos"]
            ref = jnp.where(is0, roll(7), jnp.where(is1, p3, jnp.where(is2, roll(1), roll(2))))
        else:
            ref = jnp.broadcast_to(p3[:, 3:4, :], p3.shape)
        e, nxt = jnp.where(upper, p3, ref - p3), p3 + jnp.where(upper, ref, 0.0)
    return e.reshape(CHUNK, d), nxt.reshape(CHUNK, d)


def _mixers_kernel(n_side, q_ref, lf_ref, v_ref, g_ref, mask_ref, gn_ref,
                   u_ref, gv_ref, ws_ref, bias_ref, *refs):
    o_ref, ob_ref = refs[n_side], refs[n_side + 1]
    gn = gn_ref[...]
    masks = _row_masks(HEAD_DIM)

    def chunk_rows(c):
        return pl.ds(c * CHUNK, CHUNK)

    def scores(c, st):
        rows = chunk_rows(c)
        lf = lf_ref[rows, :]
        qb = q_ref[rows, :].astype(BF16)
        kb = (1.0 - jnp.exp2(lf)).astype(BF16)
        s_diag = _dot_nt(qb, kb)
        s_lvl = []
        p = lf
        for lvl in range(SCAN_LEVELS):
            e, p = _level_exponents(p, lvl, masks)
            wb = jnp.exp2(e).astype(BF16)
            s_lvl.append(_dot_nt(qb * wb, kb * wb if lvl > 0 else kb))
        last = p[CHUNK - 1:CHUNK, :]
        qe = qb * jnp.exp2(p).astype(BF16)
        k_end = kb * jnp.exp2(last - p).astype(BF16)
        tiles = []
        for r in range(CHUNK // 8):
            sl = slice(8 * r, 8 * r + 8)
            acc = mask_ref[SCAN_LEVELS, sl, :] * s_diag[sl]
            for lvl in range(SCAN_LEVELS):
                if lvl < 3 or (r >> (lvl - 3)) & 1:
                    acc = acc + mask_ref[lvl, sl, :] * s_lvl[lvl][sl]
            tiles.append(acc)
        attn = jnp.concatenate(tiles, axis=0).astype(BF16)
        pending = (qe, attn, st.astype(BF16))
        return st * jnp.exp2(last) + _dot_tn(v_ref[rows, :], k_end), pending

    def output(c, pending):
        rows = chunk_rows(c)
        qe, attn, st_in = pending
        o = _dot_nt(qe, st_in) + _dot(attn, v_ref[rows, :])
        o_ref[rows, :] = (_rms(o) * gn * g_ref[rows, :].astype(F32)).astype(BF16)

    n_chunks = q_ref.shape[0] // CHUNK
    st = jnp.zeros((HEAD_DIM, HEAD_DIM), F32)
    pending = {}
    for c in range(n_chunks + SCAN_LAG):
        if c < n_chunks:
            st, pending[c] = scores(c, st)
        if c >= SCAN_LAG:
            output(c - SCAN_LAG, pending.pop(c - SCAN_LAG))
    _gmlp_rows(u_ref, gv_ref, ws_ref, bias_ref, ob_ref)
    for src_ref, dst_ref in zip(refs[:n_side], refs[n_side + 2:2 * n_side + 2]):
        dst_ref[...] = src_ref[...].astype(BF16)


def _mixers(p32, p16, gn, w_spatial, bias_full, side_weights, batch, seq):
    masks = _pair_masks()
    steps = batch * HEADS
    t = batch * seq
    blk = lambda seg: pl.BlockSpec((seq, HEAD_DIM), lambda b, h: (b, seg * HEADS + h))
    step_rows = lambda b, h: (b * HEADS + h, 0)
    gmlp_blk = lambda seg: pl.BlockSpec((t // steps, WIDTH), lambda b, h: (b * HEADS + h, seg))
    side_specs = [pl.BlockSpec((w.shape[0] // steps, w.shape[1]), step_rows)
                  for w in side_weights]
    outs = pl.pallas_call(
        functools.partial(_mixers_kernel, len(side_weights)),
        grid=(batch, HEADS),
        in_specs=[
            blk(SEG_Q), blk(SEG_F), blk(SEG_INP - N32_SEGS), blk(SEG_G - N32_SEGS),
            pl.BlockSpec(masks.shape, lambda b, h: (0, 0, 0)),
            pl.BlockSpec((1, HEAD_DIM), lambda b, h: (0, 0)),
            gmlp_blk(SEG_U - N32_SEGS), gmlp_blk(SEG_V - N32_SEGS),
            pl.BlockSpec(w_spatial.shape, lambda b, h: (0, 0, 0)),
            pl.BlockSpec(bias_full.shape, lambda b, h: (0, 0)),
        ] + side_specs,
        out_specs=[pl.BlockSpec((seq, HEAD_DIM), lambda b, h: (b, h)),
                   pl.BlockSpec((t // steps, WIDTH), step_rows)] + side_specs,
        out_shape=[jax.ShapeDtypeStruct((t, WIDTH), BF16), jax.ShapeDtypeStruct((t, WIDTH), BF16)]
        + [jax.ShapeDtypeStruct(w.shape, BF16) for w in side_weights],
        compiler_params=_params("parallel", "arbitrary"),
        name="mixers",
    )(p32, p32, p16, p16, jnp.asarray(masks), gn, p16, p16, w_spatial, bias_full, *side_weights)
    return outs[0], outs[1], outs[2:]


def _gmlp_rows(u_ref, v_ref, w_ref, bias_ref, o_ref):
    t = lax.broadcasted_iota(jnp.int32, (GMLP_CHUNK, GMLP_CHUNK), 0)
    s = lax.broadcasted_iota(jnp.int32, (GMLP_CHUNK, GMLP_CHUNK), 1)
    keep = (s // CHUNK) <= (t // CHUNK)
    for h in range(HEADS):
        cols = slice(h * HEAD_DIM, (h + 1) * HEAD_DIM)
        w = jnp.where(keep, w_ref[h], 0.0).astype(BF16)
        bias = bias_ref[:, cols]
        for g in range(u_ref.shape[0] // GMLP_CHUNK):
            rows = slice(g * GMLP_CHUNK, (g + 1) * GMLP_CHUNK)
            sv = _dot(w, v_ref[rows, cols]) + bias
            o_ref[rows, cols] = (u_ref[rows, cols].astype(F32) * sv).astype(BF16)


def _merge_kernel(n_side, a_ref, b_ref, ga_ref, gb_ref, x_ref, wa_ref, wb_ref, wo_ref, *refs):
    o_ref, m_ref = refs[n_side], refs[2 * n_side + 1]
    for n in range(D_MODEL // MXU_COLS):
        cols = slice(n * MXU_COLS, (n + 1) * MXU_COLS)
        ya = _dot(a_ref[...], wa_ref[:, cols])
        yb = _dot(b_ref[...], wb_ref[:, cols])
        m_ref[:, cols] = (ga_ref[:, cols].astype(F32) * ya
                          + gb_ref[:, cols].astype(F32) * yb).astype(BF16)
    for n in range(D_MODEL // MXU_COLS):
        cols = slice(n * MXU_COLS, (n + 1) * MXU_COLS)
        o_ref[:, cols] = x_ref[:, cols] + _dot(m_ref[...], wo_ref[:, cols])
    for src_ref, dst_ref in zip(refs[:n_side], refs[n_side + 1:2 * n_side + 1]):
        dst_ref[...] = src_ref[...].astype(BF16)


def _merge(oa, ob, p16, x2d, wa, wb, wo, side_weights, *, tm):
    t = x2d.shape[0]
    row = lambda i: (i, 0)
    fixed = lambda i: (0, 0)
    gate0 = (SEG_GATE - N32_SEGS) * WIDTH // D_MODEL
    steps = t // tm
    side_specs = [pl.BlockSpec((w.shape[0] // steps, w.shape[1]), row) for w in side_weights]
    outs = pl.pallas_call(
        functools.partial(_merge_kernel, len(side_weights)),
        grid=(steps,),
        in_specs=[
            pl.BlockSpec((tm, WIDTH), row),
            pl.BlockSpec((tm, WIDTH), row),
            pl.BlockSpec((tm, D_MODEL), lambda i: (i, gate0)),
            pl.BlockSpec((tm, D_MODEL), lambda i: (i, gate0 + 1)),
            pl.BlockSpec((tm, D_MODEL), row),
            pl.BlockSpec(wa.shape, fixed),
            pl.BlockSpec(wb.shape, fixed),
            pl.BlockSpec(wo.shape, fixed),
        ] + side_specs,
        out_specs=[pl.BlockSpec((tm, D_MODEL), row)] + side_specs,
        out_shape=[jax.ShapeDtypeStruct((t, D_MODEL), F32)]
        + [jax.ShapeDtypeStruct(w.shape, BF16) for w in side_weights],
        scratch_shapes=[pltpu.VMEM((tm, D_MODEL), BF16)],
        compiler_params=_params("parallel", vmem_limit_bytes=BIG_VMEM_LIMIT_BYTES),
        name="merge_out",
    )(oa, ob, p16, p16, x2d, wa, wb, wo, *side_weights)
    return outs[0], outs[1:]


def _ffn_kernel(x_ref, gain_ref, w1_ref, w2_ref, o_ref, h_ref):
    j = pl.program_id(1)

    @pl.when(j == 0)
    def _():
        x = x_ref[...]
        h_ref[...] = (_rms(x) * gain_ref[...]).astype(BF16)
        o_ref[...] = x

    a = jnp.maximum(_dot(h_ref[...], w1_ref[...]), 0.0)
    o_ref[...] += _dot((a * a).astype(BF16), w2_ref[...])


def _ffn(x2d, gain, w1, w2, *, tm, tf):
    t = x2d.shape[0]
    row = lambda i, j: (i, 0)
    return pl.pallas_call(
        _ffn_kernel,
        grid=(t // tm, D_FF // tf),
        in_specs=[
            pl.BlockSpec((tm, D_MODEL), row),
            pl.BlockSpec((1, D_MODEL), lambda i, j: (0, 0)),
            pl.BlockSpec((D_MODEL, tf), lambda i, j: (0, j)),
            pl.BlockSpec((tf, D_MODEL), lambda i, j: (j, 0)),
        ],
        out_specs=pl.BlockSpec((tm, D_MODEL), row),
        out_shape=jax.ShapeDtypeStruct((t, D_MODEL), F32),
        scratch_shapes=[pltpu.VMEM((tm, D_MODEL), BF16)],
        compiler_params=_params("parallel", "arbitrary", vmem_limit_bytes=BIG_VMEM_LIMIT_BYTES),
        name="ffn",
    )(x2d, gain, w1, w2)


def _ple_kernel(x_ref, p_ref, gain_ref, wg_ref, wp_ref, gfin_ref, o_ref, h_ref):
    h_ref[...] = (_rms(x_ref[...]) * gain_ref[...]).astype(BF16)
    pb = p_ref[...].astype(BF16)
    sumsq = jnp.zeros((x_ref.shape[0], 1), F32)
    for n in range(D_MODEL // MXU_COLS):
        cols = slice(n * MXU_COLS, (n + 1) * MXU_COLS)
        gate = _sigmoid(_dot(h_ref[...], wg_ref[:, cols]))
        y = x_ref[:, cols] + gate * _dot(pb, wp_ref[:, cols])
        o_ref[:, cols] = y
        sumsq = sumsq + jnp.sum(y * y, axis=-1, keepdims=True)
    scale = lax.rsqrt(sumsq * (1.0 / D_MODEL) + EPS)
    o_ref[...] = o_ref[...] * scale * gfin_ref[...]


def _ple(x2d, p2d, gain, wg, wp, gfin, *, tm):
    t = x2d.shape[0]
    row = lambda i: (i, 0)
    fixed = lambda i: (0, 0)
    return pl.pallas_call(
        _ple_kernel,
        grid=(t // tm,),
        in_specs=[
            pl.BlockSpec((tm, D_MODEL), row),
            pl.BlockSpec((tm, PLE_DIM), row),
            pl.BlockSpec((1, D_MODEL), fixed),
            pl.BlockSpec(wg.shape, fixed),
            pl.BlockSpec(wp.shape, fixed),
            pl.BlockSpec((1, D_MODEL), fixed),
        ],
        out_specs=pl.BlockSpec((tm, D_MODEL), row),
        out_shape=jax.ShapeDtypeStruct((t, D_MODEL), F32),
        scratch_shapes=[pltpu.VMEM((tm, D_MODEL), BF16)],
        compiler_params=_params("parallel"),
        name="ple_final",
    )(x2d, p2d, gain, wg, wp, gfin)


def kernel(x, p, norm_mix, w_in, lb_logits, hgrn_norm, w_a_out, gmlp_ln_g, gmlp_ln_b,
           w_spatial, b_spatial, w_b_out, w_o, norm_ffn, w_ff1, w_ff2, norm_ple,
           w_ple_gate, w_ple_proj, norm_final):
    batch, seq, _ = x.shape
    assert w_in.shape[0] == 1 and seq % GMLP_CHUNK == 0
    t = batch * seq
    x2d = x.reshape(t, D_MODEL)
    row = lambda a: a.reshape(1, -1)

    p32, p16 = _inproj(x2d, row(norm_mix[0]), w_in[0], lb_logits,
                       row(gmlp_ln_g[0]), row(gmlp_ln_b[0]), tm=INPROJ_ROWS)
    bias_full = jnp.repeat(b_spatial[0].T, HEAD_DIM, axis=1)
    o_a, o_b, (wa, wb, wo, w1, wg) = _mixers(
        p32, p16, row(hgrn_norm[0]), w_spatial[0], bias_full,
        [w_a_out[0], w_b_out[0], w_o[0], w_ff1[0], w_ple_gate[0]], batch, seq)
    x2d, (w2,) = _merge(o_a, o_b, p16, x2d, wa, wb, wo, [w_ff2[0]], tm=MERGE_ROWS)
    x2d = _ffn(x2d, row(norm_ffn[0]), w1, w2, tm=FFN_ROWS, tf=FFN_COLS)
    x2d = _ple(x2d, p[0].reshape(t, PLE_DIM), row(norm_ple[0]), wg,
               w_ple_proj[0].astype(BF16), row(norm_final), tm=PLE_ROWS)
    return x2d.reshape(batch, seq, D_MODEL)
```
